```python
import math
import jax, jax.numpy as jnp
from jax import lax
import numpy as np

D_MODEL = 1024
BATCH = 8
SEQ = 2048
DEPTH = 1

EPS = 1e-6
CONV_WIDTH = D_MODEL
CONV_K = 3
N_HEADS = 8
HEAD_DIM = 128
N_KV_HEADS = 2
GROUP = N_HEADS // N_KV_HEADS
ATTN_WIDTH = N_HEADS * HEAD_DIM
KV_WIDTH = N_KV_HEADS * HEAD_DIM
IDX_HEADS = 8
IDX_DIM = 64
INDEX_TOPK = 256
Q_BLOCK = 128
ATTN_SCALE = HEAD_DIM ** -0.5
IDX_SCALE = IDX_DIM ** -0.5
IDX_W_SCALE = IDX_HEADS ** -0.5
ROPE_THETA = 500000.0
ROPE_FRACTION = 4
PEER_HEADS = 8
PEER_KEY_DIM = 256
PEER_HALF = PEER_KEY_DIM // 2
N_KEYS = 128
N_EXPERTS = N_KEYS * N_KEYS
PEER_TOPK = 16
TOKEN_BLOCK = 128

COL_SIZES = (
    CONV_WIDTH,
    CONV_WIDTH,
    CONV_WIDTH,
    ATTN_WIDTH,
    KV_WIDTH,
    KV_WIDTH,
    IDX_HEADS * IDX_DIM,
    IDX_DIM,
    IDX_HEADS,
    D_MODEL,
    D_MODEL,
)
IN_COLS = sum(COL_SIZES)
SPLIT_POINTS = [int(c) for c in np.cumsum(COL_SIZES)[:-1]]

kernel_name = "hybrid_shortconv_dsa_peer_block"


def rms_norm(x, g):
    xf = x.astype(jnp.float32)
    y = xf * lax.rsqrt(jnp.mean(xf * xf, axis=-1, keepdims=True) + EPS)
    return (y * g.astype(jnp.float32)).astype(x.dtype)


def partial_rope(x, pos):
    d = x.shape[-1]
    r = d // ROPE_FRACTION
    half = r // 2
    inv_freq = ROPE_THETA ** (-jnp.arange(half, dtype=jnp.float32) / half)
    ang = pos.astype(jnp.float32)[:, None] * inv_freq[None, :]
    cos = jnp.cos(ang)[:, None, :]
    sin = jnp.sin(ang)[:, None, :]
    xf = x.astype(jnp.float32)
    x1, x2, rest = xf[..., :half], xf[..., half:r], xf[..., r:]
    out = jnp.concatenate([x1 * cos - x2 * sin, x2 * cos + x1 * sin, rest], axis=-1)
    return out.astype(x.dtype)


def short_conv_mixer(b_gate, c_gate, xt, conv_w):
    seq = xt.shape[1]
    u = c_gate * xt
    up = jnp.pad(u, ((0, 0), (CONV_K - 1, 0), (0, 0)))
    conv = sum(conv_w[j] * up[:, j:j + seq] for j in range(CONV_K))
    return b_gate * conv


def to_blocks(t, nblk):
    b = t.shape[0]
    t = t.reshape((b, nblk, Q_BLOCK) + t.shape[2:])
    return jnp.moveaxis(t, 1, 0)


def dsa_attention(q, k, v, qi, ki, wi):
    b, seq = q.shape[0], q.shape[1]
    n_keys_visible = seq
    k_sel = min(INDEX_TOPK, n_keys_visible // 4)
    nblk = seq // Q_BLOCK
    key_pos = jnp.arange(n_keys_visible)
    gather = jax.vmap(lambda tb, ib: tb[ib])

    def attend_block(args):
        qb, qib, wib, t0 = args
        tq = t0 + jnp.arange(Q_BLOCK)
        dots = jnp.einsum('bqhd,bsd->bqhs', qib, ki) * IDX_SCALE
        iscore = jnp.einsum('bqh,bqhs->bqs', wib, jax.nn.relu(dots)).astype(jnp.float32)
        causal = key_pos[None, :] <= tq[:, None]
        iscore = jnp.where(causal[None], iscore, -jnp.inf)
        _, sel = lax.top_k(iscore, k_sel)
        valid = sel <= tq[None, :, None]
        ks = gather(k, sel)
        vs = gather(v, sel)
        qg = qb.reshape(b, Q_BLOCK, N_KV_HEADS, GROUP, HEAD_DIM)
        s = jnp.einsum('bqgnd,bqkgd->bqgnk', qg, ks).astype(jnp.float32) * ATTN_SCALE
        s = jnp.where(valid[:, :, None, None, :], s, -jnp.inf)
        p = jax.nn.softmax(s, axis=-1).astype(vs.dtype)
        o = jnp.einsum('bqgnk,bqkgd->bqgnd', p, vs)
        return o.reshape(b, Q_BLOCK, ATTN_WIDTH)

    starts = jnp.arange(nblk) * Q_BLOCK
    out = lax.map(attend_block, (to_blocks(q, nblk), to_blocks(qi, nblk),
                                 to_blocks(wi, nblk), starts))
    return jnp.moveaxis(out, 0, 1).reshape(b, seq, ATTN_WIDTH)


def peer_layer(xn, peer_wq, peer_k1, peer_k2, peer_u, peer_v):
    b, seq, d = xn.shape
    xt = xn.reshape(-1, TOKEN_BLOCK, d)

    def block(xc):
        q = (xc @ peer_wq).reshape(TOKEN_BLOCK, PEER_HEADS, 2, PEER_HALF)
        s1 = jnp.einsum('thd,hnd->thn', q[:, :, 0], peer_k1).astype(jnp.float32)
        s2 = jnp.einsum('thd,hnd->thn', q[:, :, 1], peer_k2).astype(jnp.float32)
        v1, i1 = lax.top_k(s1, PEER_TOPK)
        v2, i2 = lax.top_k(s2, PEER_TOPK)
        cand = (v1[..., :, None] + v2[..., None, :]).reshape(TOKEN_BLOCK, PEER_HEADS, PEER_TOPK * PEER_TOPK)
        cidx = (i1[..., :, None] * N_KEYS + i2[..., None, :]).reshape(TOKEN_BLOCK, PEER_HEADS, PEER_TOPK * PEER_TOPK)
        top_s, pos = lax.top_k(cand, PEER_TOPK)
        eidx = jnp.take_along_axis(cidx, pos, axis=-1)
        g = jax.nn.softmax(top_s, axis=-1)
        ue = peer_u[eidx]
        ve = peer_v[eidx]
        act = jax.nn.gelu(jnp.einsum('td,thkd->thk', xc, ue).astype(jnp.float32), approximate=False)
        return jnp.einsum('thk,thkd->td', (g * act).astype(ve.dtype), ve)

    out = lax.map(block, xt)
    return out.reshape(b, seq, d)


def setup_inputs(seed: int = 0) -> dict:
    key = jax.random.key(seed)
    ks = jax.random.split(key, 14)
    f32 = jnp.float32
    x = jax.random.normal(ks[0], (BATCH, SEQ, D_MODEL), f32)
    norm1_g = 1.0 + 0.02 * jax.random.normal(ks[1], (D_MODEL,), f32)
    w_in = jax.random.normal(ks[2], (D_MODEL, IN_COLS), f32) * D_MODEL ** -0.5
    conv_w = jax.random.normal(ks[3], (CONV_K, CONV_WIDTH), f32) * CONV_K ** -0.5
    q_norm_g = 1.0 + 0.02 * jax.random.normal(ks[4], (HEAD_DIM,), f32)
    k_norm_g = 1.0 + 0.02 * jax.random.normal(ks[5], (HEAD_DIM,), f32)
    w_o = jax.random.normal(ks[6], (D_MODEL, D_MODEL), f32) * D_MODEL ** -0.5
    norm2_g = 1.0 + 0.02 * jax.random.normal(ks[7], (D_MODEL,), f32)
    peer_wq = jax.random.normal(ks[8], (D_MODEL, PEER_HEADS * PEER_KEY_DIM), f32) * D_MODEL ** -0.5
    peer_k1 = jax.random.normal(ks[9], (PEER_HEADS, N_KEYS, PEER_HALF), f32) * PEER_HALF ** -0.5
    peer_k2 = jax.random.normal(ks[10], (PEER_HEADS, N_KEYS, PEER_HALF), f32) * PEER_HALF ** -0.5
    peer_u = jax.random.normal(ks[11], (N_EXPERTS, D_MODEL), f32) * D_MODEL ** -0.5
    peer_v = jax.random.normal(ks[12], (N_EXPERTS, D_MODEL), f32) * PEER_HEADS ** -0.5
    return {"x": x, "norm1_g": norm1_g, "w_in": w_in, "conv_w": conv_w,
            "q_norm_g": q_norm_g, "k_norm_g": k_norm_g, "w_o": w_o,
            "norm2_g": norm2_g, "peer_wq": peer_wq, "peer_k1": peer_k1,
            "peer_k2": peer_k2, "peer_u": peer_u, "peer_v": peer_v}


def reference(x, norm1_g, w_in, conv_w, q_norm_g, k_norm_g, w_o, norm2_g,
              peer_wq, peer_k1, peer_k2, peer_u, peer_v):
    b, seq, _ = x.shape
    pos = jnp.arange(seq)
    h = x
    for _layer in range(DEPTH):
        xn = rms_norm(h, norm1_g)
        proj = xn @ w_in
        (b_gate, c_gate, xt, q, k, v, qi, ki, wi, ga, gb) = jnp.split(proj, SPLIT_POINTS, axis=-1)
        y_a = short_conv_mixer(b_gate, c_gate, xt, conv_w)
        q = q.reshape(b, seq, N_HEADS, HEAD_DIM)
        k = k.reshape(b, seq, N_KV_HEADS, HEAD_DIM)
        v = v.reshape(b, seq, N_KV_HEADS, HEAD_DIM)
        q = partial_rope(rms_norm(q, q_norm_g), pos)
        k = partial_rope(rms_norm(k, k_norm_g), pos)
        qi = partial_rope(qi.reshape(b, seq, IDX_HEADS, IDX_DIM), pos)
        ki = partial_rope(ki.reshape(b, seq, 1, IDX_DIM), pos)[:, :, 0]
        wi = wi * IDX_W_SCALE
        y_b = dsa_attention(q, k, v, qi, ki, wi)
        merged = jax.nn.sigmoid(ga) * y_a + jax.nn.sigmoid(gb) * y_b
        h = h + merged @ w_o
        hn = rms_norm(h, norm2_g)
        h = h + peer_layer(hn, peer_wq, peer_k1, peer_k2, peer_u, peer_v)
    return h
```

```python
import functools
import math

import jax
import jax.numpy as jnp
import numpy as np
from jax import lax
from jax.experimental import pallas as pl
from jax.experimental.pallas import tpu as pltpu

F32 = jnp.float32
BF16 = jnp.bfloat16
I32 = jnp.int32

D_MODEL = 1024
EPS = 1e-6
CONV_K = 3
N_HEADS = 8
HEAD_DIM = 128
N_KV_HEADS = 2
GROUP = N_HEADS // N_KV_HEADS
KV_WIDTH = N_KV_HEADS * HEAD_DIM
IDX_HEADS = 8
IDX_DIM = 64
INDEX_TOPK = 256
Q_BLOCK = 128
ATTN_SCALE = HEAD_DIM ** -0.5
IDX_SCALE = IDX_DIM ** -0.5
IDX_W_SCALE = IDX_HEADS ** -0.5
ROPE_THETA = 500000.0
ROPE_FRACTION = 4
PEER_HEADS = 8
PEER_HALF = 128
N_KEYS = 128
PEER_TOPK = 16

LANES = 128
INT_MIN = -(2 ** 31)
NEG_INF = float("-inf")

COL_SIZES = (1024, 1024, 1024, 1024, 256, 256, 512, 64, 8, 1024, 1024)
OFF_B, OFF_C, OFF_XT, OFF_Q, OFF_K, OFF_V, OFF_QI, OFF_GA, OFF_GB, OFF_KIWI = (
    0, 1024, 2048, 3072, 4096, 4352, 4608, 5120, 6144, 7168)
PROJ_COLS = 7296
W_SCRATCH_PITCH = 132

NT_DIMS = (((1,), (1,)), ((), ()))


def _nt_dot(a, b):
    return lax.dot_general(a, b, NT_DIMS, preferred_element_type=F32)


def _dot(a, b):
    return jnp.dot(a, b, preferred_element_type=F32)


def _in_proj_kernel(x_ref, g_ref, w_ref, o_ref):
    x = x_ref[...]
    xn = x * lax.rsqrt(jnp.mean(x * x, axis=-1, keepdims=True) + EPS) * g_ref[...]
    o_ref[...] = _dot(xn.astype(BF16), w_ref[...])


def _in_proj(x2, g, w, tm=512, tn=2432):
    n = x2.shape[0]
    return pl.pallas_call(
        _in_proj_kernel,
        grid=(PROJ_COLS // tn, n // tm),
        in_specs=[pl.BlockSpec((tm, D_MODEL), lambda j, i: (i, 0)),
                  pl.BlockSpec((1, D_MODEL), lambda j, i: (0, 0)),
                  pl.BlockSpec((D_MODEL, tn), lambda j, i: (0, j))],
        out_specs=pl.BlockSpec((tm, tn), lambda j, i: (i, j)),
        out_shape=jax.ShapeDtypeStruct((n, PROJ_COLS), F32),
        compiler_params=pltpu.CompilerParams(
            dimension_semantics=("arbitrary", "arbitrary"), vmem_limit_bytes=48 << 20),
        name="in_proj",
    )(x2, g, w)


def _rope(x, c, sa, sb, shift):
    left = pltpu.roll(x, LANES - shift, 1)
    right = pltpu.roll(x, shift, 1)
    return x * c + left * sa + right * sb


def _head_norm(x, g):
    return x * lax.rsqrt(jnp.mean(x * x, axis=-1, keepdims=True) + EPS) * g


def _prep_kernel(q_ref, k_ref, v_ref, qi_ref, kiwi_ref, t128_ref, t64_ref, tki_ref,
                 qg_ref, kg_ref,
                 qn_ref, kn_ref, vb_ref, qir_ref, kir_ref, wis_ref):
    tm = q_ref.shape[0]
    c128, sa128, sb128 = t128_ref[0], t128_ref[1], t128_ref[2]
    c64, sa64, sb64 = t64_ref[0], t64_ref[1], t64_ref[2]
    cki, saki, sbki = tki_ref[0], tki_ref[1], tki_ref[2]
    qg = qg_ref[...]
    kg = kg_ref[...]
    for h in range(N_HEADS):
        sl = slice(h * HEAD_DIM, (h + 1) * HEAD_DIM)
        qh = _head_norm(q_ref[:, sl], qg)
        qn_ref[:, sl] = _rope(qh, c128, sa128, sb128, 16).astype(BF16)
    for h in range(N_KV_HEADS):
        sl = slice(h * HEAD_DIM, (h + 1) * HEAD_DIM)
        kh = _head_norm(k_ref[:, sl], kg)
        kn_ref[:, sl] = _rope(kh, c128, sa128, sb128, 16).astype(BF16)
    vb_ref[...] = v_ref[...].astype(BF16)
    for p in range(IDX_HEADS // 2):
        sl = slice(p * LANES, (p + 1) * LANES)
        pair = _rope(qi_ref[:, sl], c64, sa64, sb64, 8).astype(BF16)
        for r in range(tm // Q_BLOCK):
            rows = slice(r * Q_BLOCK, (r + 1) * Q_BLOCK)
            qir_ref[r, 2 * p] = pair[rows, :IDX_DIM]
            qir_ref[r, 2 * p + 1] = pair[rows, IDX_DIM:]
    kiwi = kiwi_ref[...]
    kir_ref[...] = _rope(kiwi, cki, saki, sbki, 8)[:, :IDX_DIM].astype(BF16)
    wis_ref[...] = kiwi * IDX_W_SCALE


def _prep(proj, t128, t64, tki, qg, kg, seq, tm=256):
    n = proj.shape[0]
    sblk = seq // tm
    tab = lambda i: (0, i % sblk, 0)
    return pl.pallas_call(
        _prep_kernel,
        grid=(n // tm,),
        in_specs=[pl.BlockSpec((tm, 1024), lambda i: (i, OFF_Q // 1024)),
                  pl.BlockSpec((tm, 256), lambda i: (i, OFF_K // 256)),
                  pl.BlockSpec((tm, 256), lambda i: (i, OFF_V // 256)),
                  pl.BlockSpec((tm, 512), lambda i: (i, OFF_QI // 512)),
                  pl.BlockSpec((tm, 128), lambda i: (i, OFF_KIWI // 128)),
                  pl.BlockSpec((3, tm, 128), tab),
                  pl.BlockSpec((3, tm, 128), tab),
                  pl.BlockSpec((3, tm, 128), tab),
                  pl.BlockSpec((1, 128), lambda i: (0, 0)),
                  pl.BlockSpec((1, 128), lambda i: (0, 0))],
        out_specs=[pl.BlockSpec((tm, 1024), lambda i: (i, 0)),
                   pl.BlockSpec((tm, 256), lambda i: (i, 0)),
                   pl.BlockSpec((tm, 256), lambda i: (i, 0)),
                   pl.BlockSpec((tm // Q_BLOCK, IDX_HEADS, Q_BLOCK, IDX_DIM), lambda i: (i, 0, 0, 0)),
                   pl.BlockSpec((tm, IDX_DIM), lambda i: (i, 0)),
                   pl.BlockSpec((tm, 128), lambda i: (i, 0))],
        out_shape=[jax.ShapeDtypeStruct((n, 1024), BF16),
                   jax.ShapeDtypeStruct((n, 256), BF16),
                   jax.ShapeDtypeStruct((n, 256), BF16),
                   jax.ShapeDtypeStruct((n // Q_BLOCK, IDX_HEADS, Q_BLOCK, IDX_DIM), BF16),
                   jax.ShapeDtypeStruct((n, IDX_DIM), BF16),
                   jax.ShapeDtypeStruct((n, 128), F32)],
        compiler_params=pltpu.CompilerParams(dimension_semantics=("arbitrary",)),
        name="prep",
    )(proj, proj, proj, proj, proj, t128, t64, tki, qg, kg)


def _sortable_key(x):
    bits = pltpu.bitcast(x, I32)
    return bits ^ ((bits >> 31) & 0x7FFFFFFF)


def _dsa_kernel(qn_ref, kn_ref, vb_ref, qir_ref, kir_ref, wis_ref, o_ref,
                keys_ref, bias_ref, wbc_ref, mx_ref, l_ref, acc_ref, *, k_sel):
    j = pl.program_id(1)
    nch = j + 1
    qb = Q_BLOCK

    wis = wis_ref[...]
    for h in range(IDX_HEADS):
        wbc_ref[h] = jnp.broadcast_to(wis[:, IDX_DIM + h:IDX_DIM + h + 1], (qb, LANES))
    qi = qir_ref[0].reshape(IDX_HEADS * qb, IDX_DIM)
    row = lax.broadcasted_iota(I32, (qb, LANES), 0)
    col = lax.broadcasted_iota(I32, (qb, LANES), 1)

    def score_chunk(c, carry):
        kc = kir_ref[pl.ds(pl.multiple_of(c * qb, qb), qb), :]
        d = jnp.maximum(_nt_dot(qi, kc) * IDX_SCALE, 0.0).reshape(IDX_HEADS, qb, LANES)
        acc = wbc_ref[0] * d[0]
        for h in range(1, IDX_HEADS):
            acc = acc + wbc_ref[h] * d[h]
        causal = (c * qb + col) <= (j * qb + row)
        keys_ref[c] = jnp.where(causal, _sortable_key(acc), INT_MIN)
        return carry

    lax.fori_loop(0, nch, score_chunk, 0)

    kf = float(k_sel)

    def count_ge(cand):
        def body(c, cnt):
            return cnt + jnp.where(keys_ref[c] >= cand, 1.0, 0.0)
        cnt = lax.fori_loop(0, nch, body, jnp.zeros((qb, LANES), F32))
        return jnp.sum(cnt, axis=1, keepdims=True)

    t0 = jnp.where(count_ge(jnp.zeros((qb, 1), I32)) >= kf, 0, INT_MIN).astype(I32)

    def bit_body(i, t):
        cand = t | jnp.left_shift(jnp.int32(1), 30 - i)
        return jnp.where(count_ge(cand) >= kf, cand, t)

    thr = lax.fori_loop(0, 31, bit_body, t0)

    def count_gt(t):
        def body(c, cnt):
            return cnt + jnp.where(keys_ref[c] > t, 1.0, 0.0)
        cnt = lax.fori_loop(0, nch, body, jnp.zeros((qb, LANES), F32))
        return jnp.sum(cnt, axis=1, keepdims=True)

    need = kf - count_gt(thr)
    tri = jnp.where(row <= col, 1.0, 0.0).astype(BF16)

    def mask_chunk(c, seen):
        key = keys_ref[c]
        eq = jnp.where((key == thr) & (key > INT_MIN), 1.0, 0.0)
        rank = seen + _dot(eq.astype(BF16), tri)
        sel = (key > thr) | ((eq > 0.0) & (rank <= need))
        bias_ref[c] = jnp.where(sel, 0.0, NEG_INF)
        return seen + jnp.sum(eq, axis=1, keepdims=True)

    lax.fori_loop(0, nch, mask_chunk, jnp.zeros((qb, 1), F32))

    for g in range(N_KV_HEADS):
        qs = jnp.concatenate(
            [qn_ref[:, (g * GROUP + n) * HEAD_DIM:(g * GROUP + n + 1) * HEAD_DIM] for n in range(GROUP)],
            axis=0)
        gsl = slice(g * HEAD_DIM, (g + 1) * HEAD_DIM)

        def scores(c):
            rows = pl.ds(pl.multiple_of(c * qb, qb), qb)
            s = _nt_dot(qs, kn_ref[rows, gsl]) * ATTN_SCALE
            b = bias_ref[c]
            return s + jnp.concatenate([b] * GROUP, axis=0), rows

        mx_ref[...] = jnp.full(mx_ref.shape, NEG_INF, F32)

        def pass1(c, carry):
            s, _ = scores(c)
            mx_ref[...] = jnp.maximum(mx_ref[...], s)
            return carry

        lax.fori_loop(0, nch, pass1, 0)
        m = jnp.max(mx_ref[...], axis=1, keepdims=True)
        l_ref[...] = jnp.zeros(l_ref.shape, F32)
        acc_ref[...] = jnp.zeros(acc_ref.shape, F32)

        def pass2(c, carry):
            s, rows = scores(c)
            p = jnp.exp(s - m)
            l_ref[...] += p
            acc_ref[...] += _dot(p.astype(BF16), vb_ref[rows, gsl])
            return carry

        lax.fori_loop(0, nch, pass2, 0)
        o = acc_ref[...] / jnp.sum(l_ref[...], axis=1, keepdims=True)
        for n in range(GROUP):
            hsl = slice((g * GROUP + n) * HEAD_DIM, (g * GROUP + n + 1) * HEAD_DIM)
            o_ref[:, hsl] = o[n * qb:(n + 1) * qb]


def _dsa(qn, kn, vb, qir, kir, wis, batch, seq):
    n = qn.shape[0]
    nblk = seq // Q_BLOCK
    k_sel = min(INDEX_TOPK, seq // 4)
    blk = lambda b, j: (b * nblk + j, 0)
    full = lambda b, j: (b, 0)
    return pl.pallas_call(
        functools.partial(_dsa_kernel, k_sel=k_sel),
        grid=(batch, nblk),
        in_specs=[pl.BlockSpec((Q_BLOCK, 1024), blk),
                  pl.BlockSpec((seq, KV_WIDTH), full),
                  pl.BlockSpec((seq, KV_WIDTH), full),
                  pl.BlockSpec((1, IDX_HEADS, Q_BLOCK, IDX_DIM), lambda b, j: (b * nblk + j, 0, 0, 0)),
                  pl.BlockSpec((seq, IDX_DIM), full),
                  pl.BlockSpec((Q_BLOCK, 128), blk)],
        out_specs=pl.BlockSpec((Q_BLOCK, 1024), blk),
        out_shape=jax.ShapeDtypeStruct((n, 1024), F32),
        scratch_shapes=[pltpu.VMEM((nblk, Q_BLOCK, LANES), I32),
                        pltpu.VMEM((nblk, Q_BLOCK, LANES), F32),
                        pltpu.VMEM((IDX_HEADS, Q_BLOCK, LANES), F32),
                        pltpu.VMEM((GROUP * Q_BLOCK, LANES), F32),
                        pltpu.VMEM((GROUP * Q_BLOCK, LANES), F32),
                        pltpu.VMEM((GROUP * Q_BLOCK, HEAD_DIM), F32)],
        compiler_params=pltpu.CompilerParams(dimension_semantics=("arbitrary", "arbitrary")),
        name="dsa",
    )(qn, kn, vb, qir, kir, wis)


def _mix_kernel(b_ref, c_ref, xt_ref, ch_ref, xth_ref, ga_ref, gb_ref, yb_ref, x_ref,
                cw_ref, wo_ref, g2_ref, wq_ref,
                h_ref, hn_ref, pq_ref, u_ref, *, blocks_per_seq):
    tm = b_ref.shape[0]
    i = pl.program_id(0)
    first = (i % blocks_per_seq) == 0
    halo = ch_ref[...] * xth_ref[...]
    u_ref[0:8, :] = jnp.where(first, 0.0, halo)
    u = c_ref[...] * xt_ref[...]
    u_ref[8:8 + tm, :] = u
    cw = cw_ref[...]
    conv = cw[0:1, :] * u_ref[6:6 + tm, :] + cw[1:2, :] * u_ref[7:7 + tm, :] + cw[2:3, :] * u
    y_a = b_ref[...] * conv
    merged = jax.nn.sigmoid(ga_ref[...]) * y_a + jax.nn.sigmoid(gb_ref[...]) * yb_ref[...]
    h = x_ref[...] + _dot(merged.astype(BF16), wo_ref[...])
    h_ref[...] = h
    hn = (h * lax.rsqrt(jnp.mean(h * h, axis=-1, keepdims=True) + EPS) * g2_ref[...]).astype(BF16)
    hn_ref[...] = hn
    pq_ref[...] = _dot(hn, wq_ref[...]).astype(BF16)


def _mix(proj, yb, x2, conv_w, wo, g2, wq, seq, tm=256):
    n = x2.shape[0]
    col = lambda off: (lambda i: (i, off // 1024))
    halo = lambda off: (lambda i: (jnp.maximum(i * (tm // 8) - 1, 0), off // 1024))
    row = lambda i: (i, 0)
    const = lambda i: (0, 0)
    return pl.pallas_call(
        functools.partial(_mix_kernel, blocks_per_seq=seq // tm),
        grid=(n // tm,),
        in_specs=[pl.BlockSpec((tm, 1024), col(OFF_B)),
                  pl.BlockSpec((tm, 1024), col(OFF_C)),
                  pl.BlockSpec((tm, 1024), col(OFF_XT)),
                  pl.BlockSpec((8, 1024), halo(OFF_C)),
                  pl.BlockSpec((8, 1024), halo(OFF_XT)),
                  pl.BlockSpec((tm, 1024), col(OFF_GA)),
                  pl.BlockSpec((tm, 1024), col(OFF_GB)),
                  pl.BlockSpec((tm, 1024), row),
                  pl.BlockSpec((tm, 1024), row),
                  pl.BlockSpec((CONV_K, 1024), const),
                  pl.BlockSpec((1024, 1024), const),
                  pl.BlockSpec((1, 1024), const),
                  pl.BlockSpec((1024, 2048), const)],
        out_specs=[pl.BlockSpec((tm, 1024), row),
                   pl.BlockSpec((tm, 1024), row),
                   pl.BlockSpec((tm, 2048), row)],
        out_shape=[jax.ShapeDtypeStruct((n, 1024), F32),
                   jax.ShapeDtypeStruct((n, 1024), BF16),
                   jax.ShapeDtypeStruct((n, 2048), BF16)],
        scratch_shapes=[pltpu.VMEM((tm + 8, 1024), F32)],
        compiler_params=pltpu.CompilerParams(
            dimension_semantics=("arbitrary",), vmem_limit_bytes=48 << 20),
        name="mix",
    )(proj, proj, proj, proj, proj, proj, proj, yb, x2, conv_w, wo, g2, wq)


_CANDS = [(i, j) for i in range(PEER_TOPK) for j in range(PEER_TOPK) if (i + 1) * (j + 1) <= PEER_TOPK]


def _peer_topk_kernel(pq_ref, kbig_ref, i1_ref, i2_ref, g_ref,
                      s_ref, val_ref, idx_ref, c_ref, top_ref, n1_ref, n2_ref):
    tm = pq_ref.shape[0]
    hp = PEER_HEADS
    for side in range(2):
        q = pq_ref[:, side * 1024:(side + 1) * 1024]
        s_ref[...] = _nt_dot(kbig_ref[side], q).reshape(N_KEYS, hp, tm)

        def extract(i, carry):
            m = s_ref[0]
            for n in range(1, N_KEYS):
                m = jnp.maximum(m, s_ref[n])
            idx = jnp.full((hp, tm), N_KEYS, I32)
            for n in reversed(range(N_KEYS)):
                idx = jnp.where(s_ref[n] == m, n, idx)
            val_ref[side, i] = m
            idx_ref[side, i] = idx
            for n in range(N_KEYS):
                s_ref[n] = jnp.where(idx == n, NEG_INF, s_ref[n])
            return carry

        lax.fori_loop(0, PEER_TOPK, extract, 0)

    for ci, (i, j) in enumerate(_CANDS):
        c_ref[ci] = val_ref[0, i] + val_ref[1, j]
    big = PEER_TOPK * PEER_TOPK

    def pick(k, carry):
        m = c_ref[0]
        for ci in range(1, len(_CANDS)):
            m = jnp.maximum(m, c_ref[ci])
        pos = jnp.full((hp, tm), big, I32)
        for ci in reversed(range(len(_CANDS))):
            i, j = _CANDS[ci]
            pos = jnp.where(c_ref[ci] == m, i * PEER_TOPK + j, pos)
        top_ref[k] = m
        isel = pos >> 4
        jsel = pos & (PEER_TOPK - 1)
        n1 = jnp.zeros((hp, tm), I32)
        n2 = jnp.zeros((hp, tm), I32)
        for r in range(PEER_TOPK):
            n1 = jnp.where(isel == r, idx_ref[0, r], n1)
            n2 = jnp.where(jsel == r, idx_ref[1, r], n2)
        n1_ref[k] = n1
        n2_ref[k] = n2
        for ci, (i, j) in enumerate(_CANDS):
            c_ref[ci] = jnp.where(pos == i * PEER_TOPK + j, NEG_INF, c_ref[ci])
        return carry

    lax.fori_loop(0, PEER_TOPK, pick, 0)

    top = top_ref[...]
    e = jnp.exp(top - top[0:1])
    gate = e / jnp.sum(e, axis=0, keepdims=True)
    i1_ref[...] = n1_ref[...].reshape(PEER_TOPK * hp, tm).T
    i2_ref[...] = n2_ref[...].reshape(PEER_TOPK * hp, tm).T
    g_ref[...] = gate.reshape(PEER_TOPK * hp, tm).T


def _peer_topk(pq, kbig, tm=128):
    n = pq.shape[0]
    r = PEER_TOPK * PEER_HEADS
    row = lambda i: (i, 0)
    ncand = len(_CANDS)
    return pl.pallas_call(
        _peer_topk_kernel,
        grid=(n // tm,),
        in_specs=[pl.BlockSpec((tm, 2048), row),
                  pl.BlockSpec((2, 1024, 1024), lambda i: (0, 0, 0))],
        out_specs=[pl.BlockSpec((tm, r), row)] * 3,
        out_shape=[jax.ShapeDtypeStruct((n, r), I32),
                   jax.ShapeDtypeStruct((n, r), I32),
                   jax.ShapeDtypeStruct((n, r), F32)],
        scratch_shapes=[pltpu.VMEM((N_KEYS, PEER_HEADS, tm), F32),
                        pltpu.VMEM((2, PEER_TOPK, PEER_HEADS, tm), F32),
                        pltpu.VMEM((2, PEER_TOPK, PEER_HEADS, tm), I32),
                        pltpu.VMEM((ncand, PEER_HEADS, tm), F32),
                        pltpu.VMEM((PEER_TOPK, PEER_HEADS, tm), F32),
                        pltpu.VMEM((PEER_TOPK, PEER_HEADS, tm), I32),
                        pltpu.VMEM((PEER_TOPK, PEER_HEADS, tm), I32)],
        compiler_params=pltpu.CompilerParams(dimension_semantics=("arbitrary",)),
        name="peer_topk",
    )(pq, kbig)


def _peer_w_kernel(i1_ref, i2_ref, g_ref, w_ref, scr_ref):
    tb = i1_ref.shape[0]
    sub = lax.broadcasted_iota(I32, (N_KEYS, LANES), 0)

    def per_token(t, carry):
        i1 = jnp.broadcast_to(i1_ref[pl.ds(t, 1), :], (N_KEYS, LANES))
        i2 = jnp.broadcast_to(i2_ref[pl.ds(t, 1), :], (N_KEYS, LANES))
        gv = jnp.broadcast_to(g_ref[pl.ds(t, 1), :], (N_KEYS, LANES))
        a = jnp.where(sub == i1, gv, 0.0).astype(BF16)
        bt = jnp.where(sub == i2, 1.0, 0.0).astype(BF16)
        scr_ref[pl.ds(t * W_SCRATCH_PITCH, N_KEYS), :] = _nt_dot(a, bt)
        return carry

    lax.fori_loop(0, tb, per_token, 0)
    for n1 in range(N_KEYS):
        w_ref[n1] = scr_ref[pl.ds(n1, tb, stride=W_SCRATCH_PITCH), :].astype(BF16)


def _peer_w(i1, i2, g, tb=128):
    n = i1.shape[0]
    row = lambda i: (i, 0)
    return pl.pallas_call(
        _peer_w_kernel,
        grid=(n // tb,),
        in_specs=[pl.BlockSpec((tb, 128), row)] * 3,
        out_specs=pl.BlockSpec((N_KEYS, tb, LANES), lambda i: (0, i, 0)),
        out_shape=jax.ShapeDtypeStruct((N_KEYS, n, LANES), BF16),
        scratch_shapes=[pltpu.VMEM((tb * W_SCRATCH_PITCH, LANES), F32)],
        compiler_params=pltpu.CompilerParams(
            dimension_semantics=("arbitrary",), vmem_limit_bytes=48 << 20),
        name="peer_w",
    )(i1, i2, g)


def _peer_ffn_kernel(hn_ref, u_ref, v_ref, w_ref, h_ref, o_ref):
    k = pl.program_id(1)
    nsub = w_ref.shape[0]

    @pl.when(k == 0)
    def _():
        o_ref[...] = h_ref[...]

    pre = _nt_dot(hn_ref[...], u_ref[...])
    act = 0.5 * pre * (1.0 + lax.erf(pre * (1.0 / math.sqrt(2.0))))
    w = jnp.concatenate([w_ref[s] for s in range(nsub)], axis=1).astype(F32)
    o_ref[...] += _dot((w * act).astype(BF16), v_ref[...])


def _peer_ffn(hn, ub, vb, w, h, tm=512, te=1024):
    n = hn.shape[0]
    ne = ub.shape[0]
    return pl.pallas_call(
        _peer_ffn_kernel,
        grid=(n // tm, ne // te),
        in_specs=[pl.BlockSpec((tm, 1024), lambda i, k: (i, 0)),
                  pl.BlockSpec((te, 1024), lambda i, k: (k, 0)),
                  pl.BlockSpec((te, 1024), lambda i, k: (k, 0)),
                  pl.BlockSpec((te // N_KEYS, tm, LANES), lambda i, k: (k, i, 0)),
                  pl.BlockSpec((tm, 1024), lambda i, k: (i, 0))],
        out_specs=pl.BlockSpec((tm, 1024), lambda i, k: (i, 0)),
        out_shape=jax.ShapeDtypeStruct((n, 1024), F32),
        compiler_params=pltpu.CompilerParams(
            dimension_semantics=("arbitrary", "arbitrary"), vmem_limit_bytes=48 << 20),
        name="peer_ffn",
    )(hn, ub, vb, w, h)


def _rope_tables(seq, dim, width, reps):
    r = dim // ROPE_FRACTION
    half = r // 2
    inv_freq = ROPE_THETA ** (-jnp.arange(half, dtype=F32) / half)
    ang = jnp.arange(seq).astype(F32)[:, None] * inv_freq[None, :]
    cos, sin = jnp.cos(ang), jnp.sin(ang)
    rest1 = jnp.ones((seq, width - r), F32)
    rest0 = jnp.zeros((seq, width - r), F32)
    zh = jnp.zeros((seq, half), F32)
    c = jnp.concatenate([cos, cos, rest1], axis=1)
    sa = jnp.concatenate([-sin, zh, rest0], axis=1)
    sb = jnp.concatenate([zh, sin, rest0], axis=1)
    pad1 = jnp.ones((seq, LANES - width * reps), F32)
    pad0 = jnp.zeros((seq, LANES - width * reps), F32)
    c = jnp.concatenate([c] * reps + [pad1], axis=1)
    sa = jnp.concatenate([sa] * reps + [pad0], axis=1)
    sb = jnp.concatenate([sb] * reps + [pad0], axis=1)
    return jnp.stack([c, sa, sb])


def kernel(x, norm1_g, w_in, conv_w, q_norm_g, k_norm_g, w_o, norm2_g,
           peer_wq, peer_k1, peer_k2, peer_u, peer_v):
    batch, seq, d = x.shape
    n = batch * seq
    x2 = x.reshape(n, d)

    splits = [int(c) for c in np.cumsum(COL_SIZES)[:-1]]
    wb, wc, wxt, wq_, wk, wv, wqi, wki, wwi, wga, wgb = jnp.split(w_in, splits, axis=1)
    pad = jnp.zeros((d, LANES - IDX_DIM - IDX_HEADS), w_in.dtype)
    w_in_p = jnp.concatenate([wb, wc, wxt, wq_, wk, wv, wqi, wga, wgb, wki, wwi, pad],
                             axis=1).astype(BF16)

    wq_p = peer_wq.reshape(d, PEER_HEADS, 2, PEER_HALF).transpose(0, 2, 1, 3).reshape(d, 2048).astype(BF16)
    eye = jnp.eye(PEER_HEADS, dtype=peer_k1.dtype)

    def big(kk):
        return jnp.einsum('hnd,hg->nhgd', kk, eye).reshape(N_KEYS * PEER_HEADS, PEER_HEADS * PEER_HALF)

    kbig = jnp.stack([big(peer_k1), big(peer_k2)]).astype(BF16)

    t128 = _rope_tables(seq, HEAD_DIM, HEAD_DIM, 1)
    t64 = _rope_tables(seq, IDX_DIM, IDX_DIM, 2)
    tki = _rope_tables(seq, IDX_DIM, IDX_DIM, 1)

    proj = _in_proj(x2, norm1_g.reshape(1, d), w_in_p)
    qn, kn, vb, qir, kir, wis = _prep(proj, t128, t64, tki, q_norm_g.reshape(1, HEAD_DIM),
                                      k_norm_g.reshape(1, HEAD_DIM), seq)
    yb = _dsa(qn, kn, vb, qir, kir, wis, batch, seq)
    h, hn, pq = _mix(proj, yb, x2, conv_w, w_o.astype(BF16), norm2_g.reshape(1, d), wq_p, seq)
    i1, i2, g = _peer_topk(pq, kbig)
    w = _peer_w(i1, i2, g)
    out = _peer_ffn(hn, peer_u.astype(BF16), peer_v.astype(BF16), w, h)
    return out.reshape(batch, seq, d)
```

```python
import functools
import math

import jax
import jax.numpy as jnp
import numpy as np
from jax import lax
from jax.experimental import pallas as pl
from jax.experimental.pallas import tpu as pltpu

F32 = jnp.float32
BF16 = jnp.bfloat16
I32 = jnp.int32

D_MODEL = 1024
EPS = 1e-6
CONV_K = 3
N_HEADS = 8
HEAD_DIM = 128
N_KV_HEADS = 2
GROUP = N_HEADS // N_KV_HEADS
KV_WIDTH = N_KV_HEADS * HEAD_DIM
IDX_HEADS = 8
IDX_DIM = 64
INDEX_TOPK = 256
Q_BLOCK = 128
ATTN_SCALE = HEAD_DIM ** -0.5
IDX_SCALE = IDX_DIM ** -0.5
IDX_W_SCALE = IDX_HEADS ** -0.5
ROPE_THETA = 500000.0
ROPE_FRACTION = 4
PEER_HEADS = 8
PEER_HALF = 128
N_KEYS = 128
PEER_TOPK = 16

LANES = 128
INT_MIN = -(2 ** 31)
NEG_INF = float("-inf")

COL_SIZES = (1024, 1024, 1024, 1024, 256, 256, 512, 64, 8, 1024, 1024)
OFF_B, OFF_C, OFF_XT, OFF_Q, OFF_K, OFF_V, OFF_QI, OFF_GA, OFF_GB, OFF_KIWI = (
    0, 1024, 2048, 3072, 4096, 4352, 4608, 5120, 6144, 7168)
PROJ_COLS = 7296
W_SCRATCH_PITCH = 132
W_TOKENS_PER_ITER = 8

NT_DIMS = (((1,), (1,)), ((), ()))


def _nt_dot(a, b):
    return lax.dot_general(a, b, NT_DIMS, preferred_element_type=F32)


def _dot(a, b):
    return jnp.dot(a, b, preferred_element_type=F32)


def _in_proj_kernel(x_ref, g_ref, w_ref, o_ref):
    x = x_ref[...]
    xn = x * lax.rsqrt(jnp.mean(x * x, axis=-1, keepdims=True) + EPS) * g_ref[...]
    o_ref[...] = _dot(xn.astype(BF16), w_ref[...])


def _in_proj(x2, g, w, tm=512, tn=2432):
    n = x2.shape[0]
    return pl.pallas_call(
        _in_proj_kernel,
        grid=(PROJ_COLS // tn, n // tm),
        in_specs=[pl.BlockSpec((tm, D_MODEL), lambda j, i: (i, 0)),
                  pl.BlockSpec((1, D_MODEL), lambda j, i: (0, 0)),
                  pl.BlockSpec((D_MODEL, tn), lambda j, i: (0, j))],
        out_specs=pl.BlockSpec((tm, tn), lambda j, i: (i, j)),
        out_shape=jax.ShapeDtypeStruct((n, PROJ_COLS), F32),
        compiler_params=pltpu.CompilerParams(
            dimension_semantics=("arbitrary", "arbitrary"), vmem_limit_bytes=48 << 20),
        name="in_proj",
    )(x2, g, w)


def _rope(x, c, sa, sb, shift):
    left = pltpu.roll(x, LANES - shift, 1)
    right = pltpu.roll(x, shift, 1)
    return x * c + left * sa + right * sb


def _head_norm(x, g):
    return x * lax.rsqrt(jnp.mean(x * x, axis=-1, keepdims=True) + EPS) * g


def _prep_kernel(q_ref, k_ref, v_ref, qi_ref, kiwi_ref, t128_ref, t64_ref, tki_ref,
                 qg_ref, kg_ref,
                 qn_ref, kn_ref, vb_ref, qir_ref, kir_ref, wis_ref):
    tm = q_ref.shape[0]
    c128, sa128, sb128 = t128_ref[0], t128_ref[1], t128_ref[2]
    c64, sa64, sb64 = t64_ref[0], t64_ref[1], t64_ref[2]
    cki, saki, sbki = tki_ref[0], tki_ref[1], tki_ref[2]
    qg = qg_ref[...]
    kg = kg_ref[...]
    for h in range(N_HEADS):
        sl = slice(h * HEAD_DIM, (h + 1) * HEAD_DIM)
        qh = _head_norm(q_ref[:, sl], qg)
        qn_ref[:, sl] = _rope(qh, c128, sa128, sb128, 16).astype(BF16)
    for h in range(N_KV_HEADS):
        sl = slice(h * HEAD_DIM, (h + 1) * HEAD_DIM)
        kh = _head_norm(k_ref[:, sl], kg)
        kn_ref[:, sl] = _rope(kh, c128, sa128, sb128, 16).astype(BF16)
    vb_ref[...] = v_ref[...].astype(BF16)
    for p in range(IDX_HEADS // 2):
        sl = slice(p * LANES, (p + 1) * LANES)
        pair = _rope(qi_ref[:, sl], c64, sa64, sb64, 8).astype(BF16)
        for r in range(tm // Q_BLOCK):
            rows = slice(r * Q_BLOCK, (r + 1) * Q_BLOCK)
            qir_ref[r, 2 * p] = pair[rows, :IDX_DIM]
            qir_ref[r, 2 * p + 1] = pair[rows, IDX_DIM:]
    kiwi = kiwi_ref[...]
    kir_ref[...] = _rope(kiwi, cki, saki, sbki, 8)[:, :IDX_DIM].astype(BF16)
    wis_ref[...] = kiwi * IDX_W_SCALE


def _prep(proj, t128, t64, tki, qg, kg, seq, tm=256):
    n = proj.shape[0]
    sblk = seq // tm
    tab = lambda i: (0, i % sblk, 0)
    return pl.pallas_call(
        _prep_kernel,
        grid=(n // tm,),
        in_specs=[pl.BlockSpec((tm, 1024), lambda i: (i, OFF_Q // 1024)),
                  pl.BlockSpec((tm, 256), lambda i: (i, OFF_K // 256)),
                  pl.BlockSpec((tm, 256), lambda i: (i, OFF_V // 256)),
                  pl.BlockSpec((tm, 512), lambda i: (i, OFF_QI // 512)),
                  pl.BlockSpec((tm, 128), lambda i: (i, OFF_KIWI // 128)),
                  pl.BlockSpec((3, tm, 128), tab),
                  pl.BlockSpec((3, tm, 128), tab),
                  pl.BlockSpec((3, tm, 128), tab),
                  pl.BlockSpec((1, 128), lambda i: (0, 0)),
                  pl.BlockSpec((1, 128), lambda i: (0, 0))],
        out_specs=[pl.BlockSpec((tm, 1024), lambda i: (i, 0)),
                   pl.BlockSpec((tm, 256), lambda i: (i, 0)),
                   pl.BlockSpec((tm, 256), lambda i: (i, 0)),
                   pl.BlockSpec((tm // Q_BLOCK, IDX_HEADS, Q_BLOCK, IDX_DIM), lambda i: (i, 0, 0, 0)),
                   pl.BlockSpec((tm, IDX_DIM), lambda i: (i, 0)),
                   pl.BlockSpec((tm, 128), lambda i: (i, 0))],
        out_shape=[jax.ShapeDtypeStruct((n, 1024), BF16),
                   jax.ShapeDtypeStruct((n, 256), BF16),
                   jax.ShapeDtypeStruct((n, 256), BF16),
                   jax.ShapeDtypeStruct((n // Q_BLOCK, IDX_HEADS, Q_BLOCK, IDX_DIM), BF16),
                   jax.ShapeDtypeStruct((n, IDX_DIM), BF16),
                   jax.ShapeDtypeStruct((n, 128), F32)],
        compiler_params=pltpu.CompilerParams(dimension_semantics=("arbitrary",)),
        name="prep",
    )(proj, proj, proj, proj, proj, t128, t64, tki, qg, kg)


def _sortable_key(x):
    bits = pltpu.bitcast(x, I32)
    return bits ^ ((bits >> 31) & 0x7FFFFFFF)


def _dsa_kernel(qn_ref, kn_ref, vb_ref, qir_ref, kir_ref, wis_ref, o_ref,
                keys_ref, bias_ref, wbc_ref, mx_ref, l_ref, acc_ref, *, k_sel):
    j = pl.program_id(1)
    nch = j + 1
    qb = Q_BLOCK

    wis = wis_ref[...]
    for h in range(IDX_HEADS):
        wbc_ref[h] = jnp.broadcast_to(wis[:, IDX_DIM + h:IDX_DIM + h + 1], (qb, LANES))
    qi = qir_ref[0].reshape(IDX_HEADS * qb, IDX_DIM)
    row = lax.broadcasted_iota(I32, (qb, LANES), 0)
    col = lax.broadcasted_iota(I32, (qb, LANES), 1)

    def score_chunk(c, carry):
        kc = kir_ref[pl.ds(pl.multiple_of(c * qb, qb), qb), :]
        d = jnp.maximum(_nt_dot(qi, kc) * IDX_SCALE, 0.0).reshape(IDX_HEADS, qb, LANES)
        acc = wbc_ref[0] * d[0]
        for h in range(1, IDX_HEADS):
            acc = acc + wbc_ref[h] * d[h]
        causal = (c * qb + col) <= (j * qb + row)
        keys_ref[c] = jnp.where(causal, _sortable_key(acc), INT_MIN)
        return carry

    lax.fori_loop(0, nch, score_chunk, 0)

    kf = float(k_sel)

    def count_ge(cand):
        def body(c, cnt):
            return cnt + jnp.where(keys_ref[c] >= cand, 1.0, 0.0)
        cnt = lax.fori_loop(0, nch, body, jnp.zeros((qb, LANES), F32))
        return jnp.sum(cnt, axis=1, keepdims=True)

    t0 = jnp.where(count_ge(jnp.zeros((qb, 1), I32)) >= kf, 0, INT_MIN).astype(I32)

    def bit_body(i, t):
        cand = t | jnp.left_shift(jnp.int32(1), 30 - i)
        return jnp.where(count_ge(cand) >= kf, cand, t)

    thr = lax.fori_loop(0, 31, bit_body, t0)

    def count_gt(t):
        def body(c, cnt):
            return cnt + jnp.where(keys_ref[c] > t, 1.0, 0.0)
        cnt = lax.fori_loop(0, nch, body, jnp.zeros((qb, LANES), F32))
        return jnp.sum(cnt, axis=1, keepdims=True)

    need = kf - count_gt(thr)
    tri = jnp.where(row <= col, 1.0, 0.0).astype(BF16)

    def mask_chunk(c, seen):
        key = keys_ref[c]
        eq = jnp.where((key == thr) & (key > INT_MIN), 1.0, 0.0)
        rank = seen + _dot(eq.astype(BF16), tri)
        sel = (key > thr) | ((eq > 0.0) & (rank <= need))
        bias_ref[c] = jnp.where(sel, 0.0, NEG_INF)
        return seen + jnp.sum(eq, axis=1, keepdims=True)

    lax.fori_loop(0, nch, mask_chunk, jnp.zeros((qb, 1), F32))

    for g in range(N_KV_HEADS):
        qs = jnp.concatenate(
            [qn_ref[:, (g * GROUP + n) * HEAD_DIM:(g * GROUP + n + 1) * HEAD_DIM] for n in range(GROUP)],
            axis=0)
        gsl = slice(g * HEAD_DIM, (g + 1) * HEAD_DIM)

        def scores(c):
            rows = pl.ds(pl.multiple_of(c * qb, qb), qb)
            s = _nt_dot(qs, kn_ref[rows, gsl]) * ATTN_SCALE
            b = bias_ref[c]
            return s + jnp.concatenate([b] * GROUP, axis=0), rows

        mx_ref[...] = jnp.full(mx_ref.shape, NEG_INF, F32)

        def pass1(c, carry):
            s, _ = scores(c)
            mx_ref[...] = jnp.maximum(mx_ref[...], s)
            return carry

        lax.fori_loop(0, nch, pass1, 0)
        m = jnp.max(mx_ref[...], axis=1, keepdims=True)
        l_ref[...] = jnp.zeros(l_ref.shape, F32)
        acc_ref[...] = jnp.zeros(acc_ref.shape, F32)

        def pass2(c, carry):
            s, rows = scores(c)
            p = jnp.exp(s - m)
            l_ref[...] += p
            acc_ref[...] += _dot(p.astype(BF16), vb_ref[rows, gsl])
            return carry

        lax.fori_loop(0, nch, pass2, 0)
        o = acc_ref[...] / jnp.sum(l_ref[...], axis=1, keepdims=True)
        for n in range(GROUP):
            hsl = slice((g * GROUP + n) * HEAD_DIM, (g * GROUP + n + 1) * HEAD_DIM)
            o_ref[:, hsl] = o[n * qb:(n + 1) * qb]


def _dsa(qn, kn, vb, qir, kir, wis, batch, seq):
    n = qn.shape[0]
    nblk = seq // Q_BLOCK
    k_sel = min(INDEX_TOPK, seq // 4)
    blk = lambda b, j: (b * nblk + j, 0)
    full = lambda b, j: (b, 0)
    return pl.pallas_call(
        functools.partial(_dsa_kernel, k_sel=k_sel),
        grid=(batch, nblk),
        in_specs=[pl.BlockSpec((Q_BLOCK, 1024), blk),
                  pl.BlockSpec((seq, KV_WIDTH), full),
                  pl.BlockSpec((seq, KV_WIDTH), full),
                  pl.BlockSpec((1, IDX_HEADS, Q_BLOCK, IDX_DIM), lambda b, j: (b * nblk + j, 0, 0, 0)),
                  pl.BlockSpec((seq, IDX_DIM), full),
                  pl.BlockSpec((Q_BLOCK, 128), blk)],
        out_specs=pl.BlockSpec((Q_BLOCK, 1024), blk),
        out_shape=jax.ShapeDtypeStruct((n, 1024), F32),
        scratch_shapes=[pltpu.VMEM((nblk, Q_BLOCK, LANES), I32),
                        pltpu.VMEM((nblk, Q_BLOCK, LANES), F32),
                        pltpu.VMEM((IDX_HEADS, Q_BLOCK, LANES), F32),
                        pltpu.VMEM((GROUP * Q_BLOCK, LANES), F32),
                        pltpu.VMEM((GROUP * Q_BLOCK, LANES), F32),
                        pltpu.VMEM((GROUP * Q_BLOCK, HEAD_DIM), F32)],
        compiler_params=pltpu.CompilerParams(dimension_semantics=("arbitrary", "arbitrary")),
        name="dsa",
    )(qn, kn, vb, qir, kir, wis)


def _mix_kernel(b_ref, c_ref, xt_ref, ch_ref, xth_ref, ga_ref, gb_ref, yb_ref, x_ref,
                cw_ref, wo_ref, g2_ref, wq_ref,
                h_ref, hn_ref, pq_ref, u_ref, *, blocks_per_seq):
    tm = b_ref.shape[0]
    i = pl.program_id(0)
    first = (i % blocks_per_seq) == 0
    halo = ch_ref[...] * xth_ref[...]
    u_ref[0:8, :] = jnp.where(first, 0.0, halo)
    u = c_ref[...] * xt_ref[...]
    u_ref[8:8 + tm, :] = u
    cw = cw_ref[...]
    conv = cw[0:1, :] * u_ref[6:6 + tm, :] + cw[1:2, :] * u_ref[7:7 + tm, :] + cw[2:3, :] * u
    y_a = b_ref[...] * conv
    merged = jax.nn.sigmoid(ga_ref[...]) * y_a + jax.nn.sigmoid(gb_ref[...]) * yb_ref[...]
    h = x_ref[...] + _dot(merged.astype(BF16), wo_ref[...])
    h_ref[...] = h
    hn = (h * lax.rsqrt(jnp.mean(h * h, axis=-1, keepdims=True) + EPS) * g2_ref[...]).astype(BF16)
    hn_ref[...] = hn
    pq_ref[...] = _dot(hn, wq_ref[...]).astype(BF16)


def _mix(proj, yb, x2, conv_w, wo, g2, wq, seq, tm=256):
    n = x2.shape[0]
    col = lambda off: (lambda i: (i, off // 1024))
    halo = lambda off: (lambda i: (jnp.maximum(i * (tm // 8) - 1, 0), off // 1024))
    row = lambda i: (i, 0)
    const = lambda i: (0, 0)
    return pl.pallas_call(
        functools.partial(_mix_kernel, blocks_per_seq=seq // tm),
        grid=(n // tm,),
        in_specs=[pl.BlockSpec((tm, 1024), col(OFF_B)),
                  pl.BlockSpec((tm, 1024), col(OFF_C)),
                  pl.BlockSpec((tm, 1024), col(OFF_XT)),
                  pl.BlockSpec((8, 1024), halo(OFF_C)),
                  pl.BlockSpec((8, 1024), halo(OFF_XT)),
                  pl.BlockSpec((tm, 1024), col(OFF_GA)),
                  pl.BlockSpec((tm, 1024), col(OFF_GB)),
                  pl.BlockSpec((tm, 1024), row),
                  pl.BlockSpec((tm, 1024), row),
                  pl.BlockSpec((CONV_K, 1024), const),
                  pl.BlockSpec((1024, 1024), const),
                  pl.BlockSpec((1, 1024), const),
                  pl.BlockSpec((1024, 2048), const)],
        out_specs=[pl.BlockSpec((tm, 1024), row),
                   pl.BlockSpec((tm, 1024), row),
                   pl.BlockSpec((tm, 2048), row)],
        out_shape=[jax.ShapeDtypeStruct((n, 1024), F32),
                   jax.ShapeDtypeStruct((n, 1024), BF16),
                   jax.ShapeDtypeStruct((n, 2048), BF16)],
        scratch_shapes=[pltpu.VMEM((tm + 8, 1024), F32)],
        compiler_params=pltpu.CompilerParams(
            dimension_semantics=("arbitrary",), vmem_limit_bytes=48 << 20),
        name="mix",
    )(proj, proj, proj, proj, proj, proj, proj, yb, x2, conv_w, wo, g2, wq)


_CANDS = [(i, j) for i in range(PEER_TOPK) for j in range(PEER_TOPK) if (i + 1) * (j + 1) <= PEER_TOPK]


def _tree(op, xs):
    xs = list(xs)
    while len(xs) > 1:
        xs = [op(xs[i], xs[i + 1]) if i + 1 < len(xs) else xs[i] for i in range(0, len(xs), 2)]
    return xs[0]


def _peer_topk_kernel(pq_ref, kbig_ref, i1_ref, i2_ref, g_ref,
                      s_ref, val_ref, idx_ref, c_ref, top_ref, n1_ref, n2_ref):
    tm = pq_ref.shape[0]
    hp = PEER_HEADS
    for side in range(2):
        q = pq_ref[:, side * 1024:(side + 1) * 1024]
        s_ref[...] = _nt_dot(kbig_ref[side], q).reshape(N_KEYS, hp, tm)

        def extract(i, carry):
            rows = [s_ref[n] for n in range(N_KEYS)]
            m = _tree(jnp.maximum, rows)
            idx = _tree(jnp.minimum, [jnp.where(rows[n] == m, n, N_KEYS) for n in range(N_KEYS)])
            val_ref[side, i] = m
            idx_ref[side, i] = idx
            for n in range(N_KEYS):
                s_ref[n] = jnp.where(idx == n, NEG_INF, rows[n])
            return carry

        lax.fori_loop(0, PEER_TOPK, extract, 0)

    for ci, (i, j) in enumerate(_CANDS):
        c_ref[ci] = val_ref[0, i] + val_ref[1, j]
    big = PEER_TOPK * PEER_TOPK

    def pick(k, carry):
        cs = [c_ref[ci] for ci in range(len(_CANDS))]
        m = _tree(jnp.maximum, cs)
        pos = _tree(jnp.minimum, [jnp.where(cs[ci] == m, i * PEER_TOPK + j, big)
                                  for ci, (i, j) in enumerate(_CANDS)])
        top_ref[k] = m
        isel = pos >> 4
        jsel = pos & (PEER_TOPK - 1)
        zero = jnp.zeros((hp, tm), I32)
        n1_ref[k] = _tree(jnp.add, [jnp.where(isel == r, idx_ref[0, r], zero) for r in range(PEER_TOPK)])
        n2_ref[k] = _tree(jnp.add, [jnp.where(jsel == r, idx_ref[1, r], zero) for r in range(PEER_TOPK)])
        for ci, (i, j) in enumerate(_CANDS):
            c_ref[ci] = jnp.where(pos == i * PEER_TOPK + j, NEG_INF, cs[ci])
        return carry

    lax.fori_loop(0, PEER_TOPK, pick, 0)

    top = top_ref[...]
    e = jnp.exp(top - top[0:1])
    gate = e / jnp.sum(e, axis=0, keepdims=True)
    i1_ref[...] = n1_ref[...].reshape(PEER_TOPK * hp, tm).T
    i2_ref[...] = n2_ref[...].reshape(PEER_TOPK * hp, tm).T
    g_ref[...] = gate.reshape(PEER_TOPK * hp, tm).T


def _peer_topk(pq, kbig, tm=256):
    n = pq.shape[0]
    r = PEER_TOPK * PEER_HEADS
    row = lambda i: (i, 0)
    ncand = len(_CANDS)
    return pl.pallas_call(
        _peer_topk_kernel,
        grid=(n // tm,),
        in_specs=[pl.BlockSpec((tm, 2048), row),
                  pl.BlockSpec((2, 1024, 1024), lambda i: (0, 0, 0))],
        out_specs=[pl.BlockSpec((tm, r), row)] * 3,
        out_shape=[jax.ShapeDtypeStruct((n, r), I32),
                   jax.ShapeDtypeStruct((n, r), I32),
                   jax.ShapeDtypeStruct((n, r), F32)],
        scratch_shapes=[pltpu.VMEM((N_KEYS, PEER_HEADS, tm), F32),
                        pltpu.VMEM((2, PEER_TOPK, PEER_HEADS, tm), F32),
                        pltpu.VMEM((2, PEER_TOPK, PEER_HEADS, tm), I32),
                        pltpu.VMEM((ncand, PEER_HEADS, tm), F32),
                        pltpu.VMEM((PEER_TOPK, PEER_HEADS, tm), F32),
                        pltpu.VMEM((PEER_TOPK, PEER_HEADS, tm), I32),
                        pltpu.VMEM((PEER_TOPK, PEER_HEADS, tm), I32)],
        compiler_params=pltpu.CompilerParams(dimension_semantics=("arbitrary",)),
        name="peer_topk",
    )(pq, kbig)


def _peer_w_kernel(i1_ref, i2_ref, g_ref, w_ref, scr_ref):
    tb = i1_ref.shape[0]
    sub = lax.broadcasted_iota(I32, (N_KEYS, LANES), 0)

    def token_group(tg, carry):
        for u in range(W_TOKENS_PER_ITER):
            t = tg * W_TOKENS_PER_ITER + u
            i1 = jnp.broadcast_to(i1_ref[pl.ds(t, 1), :], (N_KEYS, LANES))
            i2 = jnp.broadcast_to(i2_ref[pl.ds(t, 1), :], (N_KEYS, LANES))
            gv = jnp.broadcast_to(g_ref[pl.ds(t, 1), :], (N_KEYS, LANES))
            a = jnp.where(sub == i1, gv, 0.0).astype(BF16)
            bt = jnp.where(sub == i2, 1.0, 0.0).astype(BF16)
            scr_ref[pl.ds(t * W_SCRATCH_PITCH, N_KEYS), :] = _nt_dot(a, bt)
        return carry

    lax.fori_loop(0, tb // W_TOKENS_PER_ITER, token_group, 0)
    for n1 in range(N_KEYS):
        w_ref[n1] = scr_ref[pl.ds(n1, tb, stride=W_SCRATCH_PITCH), :].astype(BF16)


def _peer_w(i1, i2, g, tb=128):
    n = i1.shape[0]
    row = lambda i: (i, 0)
    return pl.pallas_call(
        _peer_w_kernel,
        grid=(n // tb,),
        in_specs=[pl.BlockSpec((tb, 128), row)] * 3,
        out_specs=pl.BlockSpec((N_KEYS, tb, LANES), lambda i: (0, i, 0)),
        out_shape=jax.ShapeDtypeStruct((N_KEYS, n, LANES), BF16),
        scratch_shapes=[pltpu.VMEM((tb * W_SCRATCH_PITCH, LANES), F32)],
        compiler_params=pltpu.CompilerParams(
            dimension_semantics=("arbitrary",), vmem_limit_bytes=48 << 20),
        name="peer_w",
    )(i1, i2, g)


def _peer_ffn_kernel(hn_ref, u_ref, v_ref, w_ref, h_ref, o_ref):
    k = pl.program_id(1)
    nsub = w_ref.shape[0]

    @pl.when(k == 0)
    def _():
        o_ref[...] = h_ref[...]

    pre = _nt_dot(hn_ref[...], u_ref[...])
    act = 0.5 * pre * (1.0 + lax.erf(pre * (1.0 / math.sqrt(2.0))))
    w = jnp.concatenate([w_ref[s] for s in range(nsub)], axis=1).astype(F32)
    o_ref[...] += _dot((w * act).astype(BF16), v_ref[...])


def _peer_ffn(hn, ub, vb, w, h, tm=512, te=1024):
    n = hn.shape[0]
    ne = ub.shape[0]
    return pl.pallas_call(
        _peer_ffn_kernel,
        grid=(n // tm, ne // te),
        in_specs=[pl.BlockSpec((tm, 1024), lambda i, k: (i, 0)),
                  pl.BlockSpec((te, 1024), lambda i, k: (k, 0)),
                  pl.BlockSpec((te, 1024), lambda i, k: (k, 0)),
                  pl.BlockSpec((te // N_KEYS, tm, LANES), lambda i, k: (k, i, 0)),
                  pl.BlockSpec((tm, 1024), lambda i, k: (i, 0))],
        out_specs=pl.BlockSpec((tm, 1024), lambda i, k: (i, 0)),
        out_shape=jax.ShapeDtypeStruct((n, 1024), F32),
        compiler_params=pltpu.CompilerParams(
            dimension_semantics=("arbitrary", "arbitrary"), vmem_limit_bytes=48 << 20),
        name="peer_ffn",
    )(hn, ub, vb, w, h)


def _rope_tables(seq, dim, width, reps):
    r = dim // ROPE_FRACTION
    half = r // 2
    inv_freq = ROPE_THETA ** (-jnp.arange(half, dtype=F32) / half)
    ang = jnp.arange(seq).astype(F32)[:, None] * inv_freq[None, :]
    cos, sin = jnp.cos(ang), jnp.sin(ang)
    rest1 = jnp.ones((seq, width - r), F32)
    rest0 = jnp.zeros((seq, width - r), F32)
    zh = jnp.zeros((seq, half), F32)
    c = jnp.concatenate([cos, cos, rest1], axis=1)
    sa = jnp.concatenate([-sin, zh, rest0], axis=1)
    sb = jnp.concatenate([zh, sin, rest0], axis=1)
    pad1 = jnp.ones((seq, LANES - width * reps), F32)
    pad0 = jnp.zeros((seq, LANES - width * reps), F32)
    c = jnp.concatenate([c] * reps + [pad1], axis=1)
    sa = jnp.concatenate([sa] * reps + [pad0], axis=1)
    sb = jnp.concatenate([sb] * reps + [pad0], axis=1)
    return jnp.stack([c, sa, sb])


def kernel(x, norm1_g, w_in, conv_w, q_norm_g, k_norm_g, w_o, norm2_g,
           peer_wq, peer_k1, peer_k2, peer_u, peer_v):
    batch, seq, d = x.shape
    n = batch * seq
    x2 = x.reshape(n, d)

    splits = [int(c) for c in np.cumsum(COL_SIZES)[:-1]]
    wb, wc, wxt, wq_, wk, wv, wqi, wki, wwi, wga, wgb = jnp.split(w_in, splits, axis=1)
    pad = jnp.zeros((d, LANES - IDX_DIM - IDX_HEADS), w_in.dtype)
    w_in_p = jnp.concatenate([wb, wc, wxt, wq_, wk, wv, wqi, wga, wgb, wki, wwi, pad],
                             axis=1).astype(BF16)

    wq_p = peer_wq.reshape(d, PEER_HEADS, 2, PEER_HALF).transpose(0, 2, 1, 3).reshape(d, 2048).astype(BF16)
    eye = jnp.eye(PEER_HEADS, dtype=peer_k1.dtype)

    def big(kk):
        return jnp.einsum('hnd,hg->nhgd', kk, eye).reshape(N_KEYS * PEER_HEADS, PEER_HEADS * PEER_HALF)

    kbig = jnp.stack([big(peer_k1), big(peer_k2)]).astype(BF16)

    t128 = _rope_tables(seq, HEAD_DIM, HEAD_DIM, 1)
    t64 = _rope_tables(seq, IDX_DIM, IDX_DIM, 2)
    tki = _rope_tables(seq, IDX_DIM, IDX_DIM, 1)

    proj = _in_proj(x2, norm1_g.reshape(1, d), w_in_p)
    qn, kn, vb, qir, kir, wis = _prep(proj, t128, t64, tki, q_norm_g.reshape(1, HEAD_DIM),
                                      k_norm_g.reshape(1, HEAD_DIM), seq)
    yb = _dsa(qn, kn, vb, qir, kir, wis, batch, seq)
    h, hn, pq = _mix(proj, yb, x2, conv_w, w_o.astype(BF16), norm2_g.reshape(1, d), wq_p, seq)
    i1, i2, g = _peer_topk(pq, kbig)
    w = _peer_w(i1, i2, g)
    out = _peer_ffn(hn, peer_u.astype(BF16), peer_v.astype(BF16), w, h)
    return out.reshape(batch, seq, d)
```

```python
import functools
import math

import jax
import jax.numpy as jnp
import numpy as np
from jax import lax
from jax.experimental import pallas as pl
from jax.experimental.pallas import tpu as pltpu

F32 = jnp.float32
BF16 = jnp.bfloat16
I32 = jnp.int32

D_MODEL = 1024
EPS = 1e-6
CONV_K = 3
N_HEADS = 8
HEAD_DIM = 128
N_KV_HEADS = 2
GROUP = N_HEADS // N_KV_HEADS
KV_WIDTH = N_KV_HEADS * HEAD_DIM
IDX_HEADS = 8
IDX_DIM = 64
INDEX_TOPK = 256
ATTN_SCALE = HEAD_DIM ** -0.5
IDX_SCALE = IDX_DIM ** -0.5
IDX_W_SCALE = IDX_HEADS ** -0.5
ROPE_THETA = 500000.0
ROPE_FRACTION = 4
PEER_HEADS = 8
PEER_HALF = 128
N_KEYS = 128
PEER_TOPK = 16

LANES = 128
INT_MIN = -(2 ** 31)
KEY_NEG_INF = INT_MIN + 0x7FFFFF
NEG_INF = float("-inf")

COL_SIZES = (1024, 1024, 1024, 1024, 256, 256, 512, 64, 8, 1024, 1024)
OFF_B, OFF_C, OFF_XT, OFF_Q, OFF_K, OFF_V, OFF_QI, OFF_GA, OFF_GB, OFF_KIWI = (
    0, 1024, 2048, 3072, 4096, 4352, 4608, 5120, 6144, 7168)
PROJ_COLS = 7296
W_SCRATCH_PITCH = 132
W_TOKENS_PER_ITER = 32
DSA_BLOCK = 256
M_FLOOR = -1e30

NT_DIMS = (((1,), (1,)), ((), ()))


def _nt_dot(a, b):
    return lax.dot_general(a, b, NT_DIMS, preferred_element_type=F32)


def _dot(a, b):
    return jnp.dot(a, b, preferred_element_type=F32)


def _in_proj_kernel(x_ref, g_ref, w_ref, o_ref):
    x = x_ref[...]
    xn = x * lax.rsqrt(jnp.mean(x * x, axis=-1, keepdims=True) + EPS) * g_ref[...]
    o_ref[...] = _dot(xn.astype(BF16), w_ref[...])


def _in_proj(x2, g, w, tm=512, tn=2432):
    n = x2.shape[0]
    return pl.pallas_call(
        _in_proj_kernel,
        grid=(PROJ_COLS // tn, n // tm),
        in_specs=[pl.BlockSpec((tm, D_MODEL), lambda j, i: (i, 0)),
                  pl.BlockSpec((1, D_MODEL), lambda j, i: (0, 0)),
                  pl.BlockSpec((D_MODEL, tn), lambda j, i: (0, j))],
        out_specs=pl.BlockSpec((tm, tn), lambda j, i: (i, j)),
        out_shape=jax.ShapeDtypeStruct((n, PROJ_COLS), F32),
        compiler_params=pltpu.CompilerParams(
            dimension_semantics=("arbitrary", "arbitrary"), vmem_limit_bytes=48 << 20),
        name="in_proj",
    )(x2, g, w)


def _rope(x, c, sa, sb, shift):
    left = pltpu.roll(x, LANES - shift, 1)
    right = pltpu.roll(x, shift, 1)
    return x * c + left * sa + right * sb


def _head_norm(x, g):
    return x * lax.rsqrt(jnp.mean(x * x, axis=-1, keepdims=True) + EPS) * g


def _prep_kernel(q_ref, k_ref, v_ref, qi_ref, kiwi_ref, t128_ref, t64_ref, tki_ref,
                 qg_ref, kg_ref,
                 qn_ref, kn_ref, vt_ref, qir_ref, kir_ref, wist_ref):
    c128, sa128, sb128 = t128_ref[0], t128_ref[1], t128_ref[2]
    c64, sa64, sb64 = t64_ref[0], t64_ref[1], t64_ref[2]
    cki, saki, sbki = tki_ref[0], tki_ref[1], tki_ref[2]
    qg = qg_ref[...]
    kg = kg_ref[...]
    for h in range(N_HEADS):
        sl = slice(h * HEAD_DIM, (h + 1) * HEAD_DIM)
        qh = _head_norm(q_ref[:, sl], qg)
        qn_ref[:, sl] = _rope(qh, c128, sa128, sb128, 16).astype(BF16)
    for h in range(N_KV_HEADS):
        sl = slice(h * HEAD_DIM, (h + 1) * HEAD_DIM)
        kh = _head_norm(k_ref[:, sl], kg)
        kn_ref[:, sl] = _rope(kh, c128, sa128, sb128, 16).astype(BF16)
    vt_ref[0] = v_ref[...].T.astype(BF16)
    for p in range(IDX_HEADS // 2):
        sl = slice(p * LANES, (p + 1) * LANES)
        pair = _rope(qi_ref[:, sl], c64, sa64, sb64, 8).astype(BF16)
        qir_ref[0, 2 * p] = pair[:, :IDX_DIM]
        qir_ref[0, 2 * p + 1] = pair[:, IDX_DIM:]
    kiwi = kiwi_ref[...]
    kir_ref[...] = _rope(kiwi, cki, saki, sbki, 8)[:, :IDX_DIM].astype(BF16)
    wist_ref[...] = (kiwi * IDX_W_SCALE).T[IDX_DIM:IDX_DIM + IDX_HEADS, :]


def _prep(proj, t128, t64, tki, qg, kg, seq):
    tm = DSA_BLOCK
    n = proj.shape[0]
    sblk = seq // tm
    tab = lambda i: (0, i % sblk, 0)
    return pl.pallas_call(
        _prep_kernel,
        grid=(n // tm,),
        in_specs=[pl.BlockSpec((tm, 1024), lambda i: (i, OFF_Q // 1024)),
                  pl.BlockSpec((tm, 256), lambda i: (i, OFF_K // 256)),
                  pl.BlockSpec((tm, 256), lambda i: (i, OFF_V // 256)),
                  pl.BlockSpec((tm, 512), lambda i: (i, OFF_QI // 512)),
                  pl.BlockSpec((tm, 128), lambda i: (i, OFF_KIWI // 128)),
                  pl.BlockSpec((3, tm, 128), tab),
                  pl.BlockSpec((3, tm, 128), tab),
                  pl.BlockSpec((3, tm, 128), tab),
                  pl.BlockSpec((1, 128), lambda i: (0, 0)),
                  pl.BlockSpec((1, 128), lambda i: (0, 0))],
        out_specs=[pl.BlockSpec((tm, 1024), lambda i: (i, 0)),
                   pl.BlockSpec((tm, KV_WIDTH), lambda i: (i, 0)),
                   pl.BlockSpec((1, KV_WIDTH, tm), lambda i: (i, 0, 0)),
                   pl.BlockSpec((1, IDX_HEADS, tm, IDX_DIM), lambda i: (i, 0, 0, 0)),
                   pl.BlockSpec((tm, IDX_DIM), lambda i: (i, 0)),
                   pl.BlockSpec((IDX_HEADS, tm), lambda i: (0, i))],
        out_shape=[jax.ShapeDtypeStruct((n, 1024), BF16),
                   jax.ShapeDtypeStruct((n, KV_WIDTH), BF16),
                   jax.ShapeDtypeStruct((n // tm, KV_WIDTH, tm), BF16),
                   jax.ShapeDtypeStruct((n // tm, IDX_HEADS, tm, IDX_DIM), BF16),
                   jax.ShapeDtypeStruct((n, IDX_DIM), BF16),
                   jax.ShapeDtypeStruct((IDX_HEADS, n), F32)],
        compiler_params=pltpu.CompilerParams(dimension_semantics=("arbitrary",)),
        name="prep",
    )(proj, proj, proj, proj, proj, t128, t64, tki, qg, kg)


def _dsa_kernel(qn_ref, kn_ref, vt_ref, qir_ref, kir_ref, wist_ref, o_ref,
                keys_ref, bias_ref, acc_ref, *, k_sel):
    j = pl.program_id(1)
    nch = j + 1
    blk = DSA_BLOCK
    kf = float(k_sel)

    def rowsum(x):
        return jnp.sum(x, axis=0, keepdims=True)

    wist = wist_ref[...]
    qi = qir_ref[0].reshape(IDX_HEADS * blk, IDX_DIM)
    diff = lax.broadcasted_iota(I32, (blk, blk), 0) - lax.broadcasted_iota(I32, (blk, blk), 1)

    def score_chunk(c, carry):
        kc = kir_ref[pl.ds(pl.multiple_of(c * blk, blk), blk), :]
        d = jnp.maximum(_nt_dot(kc, qi) * IDX_SCALE, 0.0)
        acc = wist[0:1, :] * d[:, 0:blk]
        for h in range(1, IDX_HEADS):
            acc = acc + wist[h:h + 1, :] * d[:, h * blk:(h + 1) * blk]
        keys_ref[c] = jnp.where(diff <= (j - c) * blk, acc, NEG_INF)
        return carry

    lax.fori_loop(0, nch, score_chunk, 0)

    def count(pred):
        def body(c, cnt):
            m = jnp.where(pred(keys_ref[c]), 1.0, 0.0)
            return cnt + jnp.sum(m.reshape(blk // 8, 8, blk), axis=0)
        return rowsum(lax.fori_loop(0, nch, body, jnp.zeros((8, blk), F32)))

    def as_float(t):
        t = jnp.maximum(t, KEY_NEG_INF)
        return pltpu.bitcast(t ^ ((t >> 31) & 0x7FFFFFFF), F32)

    t0 = jnp.where(count(lambda k: k >= 0.0) >= kf, 0, INT_MIN).astype(I32)

    def bit_body(i, t):
        cand = t | jnp.left_shift(jnp.int32(1), 30 - i)
        cf = as_float(cand)
        return jnp.where(count(lambda k: k >= cf) >= kf, cand, t)

    thr = as_float(lax.fori_loop(0, 31, bit_body, t0))

    need = kf - count(lambda k: k > thr)
    tri = jnp.where(diff >= 0, 1.0, 0.0).astype(BF16)

    def mask_chunk(c, seen):
        key = keys_ref[c]
        eq = jnp.where(key == thr, 1.0, 0.0)
        rank = seen + _dot(tri, eq.astype(BF16))
        tie_bias = jnp.where(rank <= need, 0.0, NEG_INF)
        eq_bias = jnp.where(eq > 0.0, tie_bias, NEG_INF)
        bias = jnp.where(key > thr, 0.0, eq_bias)
        bias_ref[c] = jnp.where(diff <= (j - c) * blk, bias, NEG_INF)
        return seen + rowsum(eq)

    lax.fori_loop(0, nch, mask_chunk, jnp.zeros((1, blk), F32))

    for g in range(N_KV_HEADS):
        qs = jnp.concatenate(
            [qn_ref[:, (g * GROUP + n) * HEAD_DIM:(g * GROUP + n + 1) * HEAD_DIM] for n in range(GROUP)],
            axis=0)
        gsl = slice(g * HEAD_DIM, (g + 1) * HEAD_DIM)
        acc_ref[...] = jnp.zeros(acc_ref.shape, F32)

        def body(c, carry):
            m, l = carry
            kc = kn_ref[pl.ds(pl.multiple_of(c * blk, blk), blk), gsl]
            s = _nt_dot(kc, qs) * ATTN_SCALE
            s = s + jnp.concatenate([bias_ref[c]] * GROUP, axis=1)
            m_new = jnp.maximum(m, jnp.max(s, axis=0, keepdims=True))
            alpha = jnp.exp(m - m_new)
            p = jnp.exp(s - m_new)
            acc_ref[...] = acc_ref[...] * alpha + _dot(vt_ref[c, gsl, :], p.astype(BF16))
            return m_new, l * alpha + rowsum(p)

        m0 = jnp.full((1, GROUP * blk), M_FLOOR, F32)
        _, l = lax.fori_loop(0, nch, body, (m0, jnp.zeros((1, GROUP * blk), F32)))
        ot = acc_ref[...] / l
        for n in range(GROUP):
            hsl = slice((g * GROUP + n) * HEAD_DIM, (g * GROUP + n + 1) * HEAD_DIM)
            o_ref[:, hsl] = ot[:, n * blk:(n + 1) * blk].T


def _dsa(qn, kn, vt, qir, kir, wist, batch, seq):
    blk = DSA_BLOCK
    n = qn.shape[0]
    nblk = seq // blk
    k_sel = min(INDEX_TOPK, seq // 4)
    qblk = lambda b, j: (b * nblk + j, 0)
    full = lambda b, j: (b, 0)
    return pl.pallas_call(
        functools.partial(_dsa_kernel, k_sel=k_sel),
        grid=(batch, nblk),
        in_specs=[pl.BlockSpec((blk, 1024), qblk),
                  pl.BlockSpec((seq, KV_WIDTH), full),
                  pl.BlockSpec((nblk, KV_WIDTH, blk), lambda b, j: (b, 0, 0)),
                  pl.BlockSpec((1, IDX_HEADS, blk, IDX_DIM), lambda b, j: (b * nblk + j, 0, 0, 0)),
                  pl.BlockSpec((seq, IDX_DIM), full),
                  pl.BlockSpec((IDX_HEADS, blk), lambda b, j: (0, b * nblk + j))],
        out_specs=pl.BlockSpec((blk, 1024), qblk),
        out_shape=jax.ShapeDtypeStruct((n, 1024), F32),
        scratch_shapes=[pltpu.VMEM((nblk, blk, blk), F32),
                        pltpu.VMEM((nblk, blk, blk), F32),
                        pltpu.VMEM((HEAD_DIM, GROUP * blk), F32)],
        compiler_params=pltpu.CompilerParams(
            dimension_semantics=("arbitrary", "arbitrary"), vmem_limit_bytes=48 << 20),
        name="dsa",
    )(qn, kn, vt, qir, kir, wist)


def _mix_kernel(b_ref, c_ref, xt_ref, ch_ref, xth_ref, ga_ref, gb_ref, yb_ref, x_ref,
                cw_ref, wo_ref, g2_ref, wq_ref,
                h_ref, hn_ref, pq_ref, u_ref, *, blocks_per_seq):
    tm = b_ref.shape[0]
    i = pl.program_id(0)
    first = (i % blocks_per_seq) == 0
    halo = ch_ref[...] * xth_ref[...]
    u_ref[0:8, :] = jnp.where(first, 0.0, halo)
    u = c_ref[...] * xt_ref[...]
    u_ref[8:8 + tm, :] = u
    cw = cw_ref[...]
    conv = cw[0:1, :] * u_ref[6:6 + tm, :] + cw[1:2, :] * u_ref[7:7 + tm, :] + cw[2:3, :] * u
    y_a = b_ref[...] * conv
    merged = jax.nn.sigmoid(ga_ref[...]) * y_a + jax.nn.sigmoid(gb_ref[...]) * yb_ref[...]
    h = x_ref[...] + _dot(merged.astype(BF16), wo_ref[...])
    h_ref[...] = h
    hn = (h * lax.rsqrt(jnp.mean(h * h, axis=-1, keepdims=True) + EPS) * g2_ref[...]).astype(BF16)
    hn_ref[...] = hn
    pq_ref[...] = _dot(hn, wq_ref[...]).astype(BF16)


def _mix(proj, yb, x2, conv_w, wo, g2, wq, seq, tm=256):
    n = x2.shape[0]
    col = lambda off: (lambda i: (i, off // 1024))
    halo = lambda off: (lambda i: (jnp.maximum(i * (tm // 8) - 1, 0), off // 1024))
    row = lambda i: (i, 0)
    const = lambda i: (0, 0)
    return pl.pallas_call(
        functools.partial(_mix_kernel, blocks_per_seq=seq // tm),
        grid=(n // tm,),
        in_specs=[pl.BlockSpec((tm, 1024), col(OFF_B)),
                  pl.BlockSpec((tm, 1024), col(OFF_C)),
                  pl.BlockSpec((tm, 1024), col(OFF_XT)),
                  pl.BlockSpec((8, 1024), halo(OFF_C)),
                  pl.BlockSpec((8, 1024), halo(OFF_XT)),
                  pl.BlockSpec((tm, 1024), col(OFF_GA)),
                  pl.BlockSpec((tm, 1024), col(OFF_GB)),
                  pl.BlockSpec((tm, 1024), row),
                  pl.BlockSpec((tm, 1024), row),
                  pl.BlockSpec((CONV_K, 1024), const),
                  pl.BlockSpec((1024, 1024), const),
                  pl.BlockSpec((1, 1024), const),
                  pl.BlockSpec((1024, 2048), const)],
        out_specs=[pl.BlockSpec((tm, 1024), row),
                   pl.BlockSpec((tm, 1024), row),
                   pl.BlockSpec((tm, 2048), row)],
        out_shape=[jax.ShapeDtypeStruct((n, 1024), F32),
                   jax.ShapeDtypeStruct((n, 1024), BF16),
                   jax.ShapeDtypeStruct((n, 2048), BF16)],
        scratch_shapes=[pltpu.VMEM((tm + 8, 1024), F32)],
        compiler_params=pltpu.CompilerParams(
            dimension_semantics=("arbitrary",), vmem_limit_bytes=48 << 20),
        name="mix",
    )(proj, proj, proj, proj, proj, proj, proj, yb, x2, conv_w, wo, g2, wq)


_CANDS = [(i, j) for i in range(PEER_TOPK) for j in range(PEER_TOPK) if (i + 1) * (j + 1) <= PEER_TOPK]


def _tree(op, xs):
    xs = list(xs)
    while len(xs) > 1:
        xs = [op(xs[i], xs[i + 1]) if i + 1 < len(xs) else xs[i] for i in range(0, len(xs), 2)]
    return xs[0]


def _oddeven_merge(lo, hi, r):
    step = r * 2
    if step < hi - lo:
        yield from _oddeven_merge(lo, hi, step)
        yield from _oddeven_merge(lo + r, hi, step)
        yield from [(i, i + r) for i in range(lo + r, hi - r, step)]
    else:
        yield (lo, lo + r)


def _oddeven_merge_sort(lo, hi):
    if hi - lo >= 1:
        mid = lo + (hi - lo) // 2
        yield from _oddeven_merge_sort(lo, mid)
        yield from _oddeven_merge_sort(mid + 1, hi)
        yield from _oddeven_merge(lo, hi, 1)


_SORT16 = list(_oddeven_merge_sort(0, PEER_TOPK - 1))
_BITONIC16 = [(i, i + d) for d in (8, 4, 2, 1) for i in range(PEER_TOPK) if not i & d]


def _hi(a, b):
    swap = b[0] > a[0]
    return (jnp.maximum(a[0], b[0]),) + tuple(jnp.where(swap, y, x) for x, y in zip(a[1:], b[1:]))


def _ce(a, b):
    swap = b[0] > a[0]
    lo = (jnp.minimum(a[0], b[0]),) + tuple(jnp.where(swap, x, y) for x, y in zip(a[1:], b[1:]))
    return _hi(a, b), lo


def _apply(items, pairs):
    for i, j in pairs:
        items[i], items[j] = _ce(items[i], items[j])
    return items


def _merge_top(a, b):
    m = len(b)
    c = [a[i] if i < PEER_TOPK - m else _hi(a[i], b[PEER_TOPK - 1 - i]) for i in range(PEER_TOPK)]
    return _apply(c, _BITONIC16)


def _peer_topk_kernel(pq_ref, kbig_ref, i1_ref, i2_ref, g_ref,
                      s_ref, val_ref, idx_ref, c_ref, top_ref, n1_ref, n2_ref):
    tm = pq_ref.shape[0]
    hp = PEER_HEADS
    halves = [slice(u * LANES, (u + 1) * LANES) for u in range(tm // LANES)]

    def any_true(flags):
        return jnp.max(_tree(jnp.maximum, flags)) > 0.0

    def tie_flag(top_vals, all_vals, ordered):
        cnt = _tree(jnp.add, [jnp.where(v >= top_vals[PEER_TOPK - 1], 1.0, 0.0) for v in all_vals])
        bad = jnp.where(cnt == float(PEER_TOPK), 0.0, 1.0)
        if ordered:
            bad = _tree(jnp.maximum, [bad] + [jnp.where(top_vals[r] > top_vals[r + 1], 0.0, 1.0)
                                              for r in range(PEER_TOPK - 1)])
        return bad

    for side in range(2):
        q = pq_ref[:, side * 1024:(side + 1) * 1024]
        s_ref[...] = _nt_dot(kbig_ref[side], q).reshape(N_KEYS, hp, tm)

        flags = []
        for ln in halves:
            rows = [s_ref[n, :, ln] for n in range(N_KEYS)]
            groups = [_apply([(rows[g * PEER_TOPK + r], g * PEER_TOPK + r) for r in range(PEER_TOPK)], _SORT16)
                      for g in range(N_KEYS // PEER_TOPK)]
            while len(groups) > 1:
                groups = [_merge_top(groups[u], groups[u + 1]) for u in range(0, len(groups), 2)]
            for r in range(PEER_TOPK):
                val_ref[side, r, :, ln] = groups[0][r][0]
                idx_ref[side, r, :, ln] = groups[0][r][1]
            flags.append(tie_flag([it[0] for it in groups[0]], rows, ordered=True))

        @pl.when(any_true(flags))
        def _():
            def extract(i, carry):
                rows = [s_ref[n] for n in range(N_KEYS)]
                m = _tree(jnp.maximum, rows)
                idx = _tree(jnp.minimum, [jnp.where(rows[n] == m, n, N_KEYS) for n in range(N_KEYS)])
                val_ref[side, i] = m
                idx_ref[side, i] = idx
                for n in range(N_KEYS):
                    s_ref[n] = jnp.where(idx == n, NEG_INF, rows[n])
                return carry

            lax.fori_loop(0, PEER_TOPK, extract, 0)

    flags = []
    for ln in halves:
        def cand(i, j):
            return (val_ref[0, i, :, ln] + val_ref[1, j, :, ln], idx_ref[0, i, :, ln], idx_ref[1, j, :, ln])

        g0 = [cand(0, j) for j in range(16)]
        g1 = _apply([cand(1, j) for j in range(8)] + [cand(i, 0) for i in range(15, 7, -1)], _BITONIC16)
        g2 = _apply([cand(i, j) for i, w in ((2, 5), (3, 4), (4, 3), (5, 2), (6, 2)) for j in range(w)], _SORT16)
        g3 = [cand(7, 0), cand(7, 1)]
        top = _merge_top(_merge_top(g0, g1), _merge_top(g2, g3))
        for k in range(PEER_TOPK):
            top_ref[k, :, ln] = top[k][0]
            n1_ref[k, :, ln] = top[k][1]
            n2_ref[k, :, ln] = top[k][2]
        flags.append(tie_flag([it[0] for it in top], [cand(i, j)[0] for i, j in _CANDS], ordered=False))

    @pl.when(any_true(flags))
    def _():
        for ci, (i, j) in enumerate(_CANDS):
            c_ref[ci] = val_ref[0, i] + val_ref[1, j]
        big = PEER_TOPK * PEER_TOPK

        def pick(k, carry):
            cs = [c_ref[ci] for ci in range(len(_CANDS))]
            m = _tree(jnp.maximum, cs)
            pos = _tree(jnp.minimum, [jnp.where(cs[ci] == m, i * PEER_TOPK + j, big)
                                      for ci, (i, j) in enumerate(_CANDS)])
            top_ref[k] = m
            isel = pos >> 4
            jsel = pos & (PEER_TOPK - 1)
            zero = jnp.zeros((hp, tm), I32)
            n1_ref[k] = _tree(jnp.add, [jnp.where(isel == r, idx_ref[0, r], zero) for r in range(PEER_TOPK)])
            n2_ref[k] = _tree(jnp.add, [jnp.where(jsel == r, idx_ref[1, r], zero) for r in range(PEER_TOPK)])
            for ci, (i, j) in enumerate(_CANDS):
                c_ref[ci] = jnp.where(pos == i * PEER_TOPK + j, NEG_INF, cs[ci])
            return carry

        lax.fori_loop(0, PEER_TOPK, pick, 0)

    top = top_ref[...]
    e = jnp.exp(top - top[0:1])
    gate = e / jnp.sum(e, axis=0, keepdims=True)
    i1_ref[...] = n1_ref[...].reshape(PEER_TOPK * hp, tm).T
    i2_ref[...] = n2_ref[...].reshape(PEER_TOPK * hp, tm).T
    g_ref[...] = gate.reshape(PEER_TOPK * hp, tm).T


def _peer_topk(pq, kbig, tm=256):
    n = pq.shape[0]
    r = PEER_TOPK * PEER_HEADS
    row = lambda i: (i, 0)
    ncand = len(_CANDS)
    return pl.pallas_call(
        _peer_topk_kernel,
        grid=(n // tm,),
        in_specs=[pl.BlockSpec((tm, 2048), row),
                  pl.BlockSpec((2, 1024, 1024), lambda i: (0, 0, 0))],
        out_specs=[pl.BlockSpec((tm, r), row)] * 3,
        out_shape=[jax.ShapeDtypeStruct((n, r), I32),
                   jax.ShapeDtypeStruct((n, r), I32),
                   jax.ShapeDtypeStruct((n, r), F32)],
        scratch_shapes=[pltpu.VMEM((N_KEYS, PEER_HEADS, tm), F32),
                        pltpu.VMEM((2, PEER_TOPK, PEER_HEADS, tm), F32),
                        pltpu.VMEM((2, PEER_TOPK, PEER_HEADS, tm), I32),
                        pltpu.VMEM((ncand, PEER_HEADS, tm), F32),
                        pltpu.VMEM((PEER_TOPK, PEER_HEADS, tm), F32),
                        pltpu.VMEM((PEER_TOPK, PEER_HEADS, tm), I32),
                        pltpu.VMEM((PEER_TOPK, PEER_HEADS, tm), I32)],
        compiler_params=pltpu.CompilerParams(dimension_semantics=("arbitrary",)),
        name="peer_topk",
    )(pq, kbig)


def _peer_w_kernel(i1_ref, i2_ref, g_ref, w_ref, scr_ref):
    tb = i1_ref.shape[0]
    sub = lax.broadcasted_iota(I32, (N_KEYS, LANES), 0)

    def token_group(tg, carry):
        for u in range(W_TOKENS_PER_ITER):
            t = tg * W_TOKENS_PER_ITER + u
            i1 = jnp.broadcast_to(i1_ref[pl.ds(t, 1), :], (N_KEYS, LANES))
            i2 = jnp.broadcast_to(i2_ref[pl.ds(t, 1), :], (N_KEYS, LANES))
            gv = jnp.broadcast_to(g_ref[pl.ds(t, 1), :], (N_KEYS, LANES))
            a = jnp.where(sub == i1, gv, 0.0).astype(BF16)
            bt = jnp.where(sub == i2, 1.0, 0.0).astype(BF16)
            scr_ref[pl.ds(t * W_SCRATCH_PITCH, N_KEYS), :] = _nt_dot(a, bt)
        return carry

    lax.fori_loop(0, tb // W_TOKENS_PER_ITER, token_group, 0)
    for n1 in range(N_KEYS):
        w_ref[n1] = scr_ref[pl.ds(n1, tb, stride=W_SCRATCH_PITCH), :].astype(BF16)


def _peer_w(i1, i2, g, tb=128):
    n = i1.shape[0]
    row = lambda i: (i, 0)
    return pl.pallas_call(
        _peer_w_kernel,
        grid=(n // tb,),
        in_specs=[pl.BlockSpec((tb, 128), row)] * 3,
        out_specs=pl.BlockSpec((N_KEYS, tb, LANES), lambda i: (0, i, 0)),
        out_shape=jax.ShapeDtypeStruct((N_KEYS, n, LANES), BF16),
        scratch_shapes=[pltpu.VMEM((tb * W_SCRATCH_PITCH, LANES), F32)],
        compiler_params=pltpu.CompilerParams(
            dimension_semantics=("arbitrary",), vmem_limit_bytes=48 << 20),
        name="peer_w",
    )(i1, i2, g)


def _peer_ffn_kernel(hn_ref, u_ref, v_ref, w_ref, h_ref, o_ref):
    k = pl.program_id(1)
    nsub = w_ref.shape[0]

    @pl.when(k == 0)
    def _():
        o_ref[...] = h_ref[...]

    pre = _nt_dot(hn_ref[...], u_ref[...])
    act = 0.5 * pre * (1.0 + lax.erf(pre * (1.0 / math.sqrt(2.0))))
    w = jnp.concatenate([w_ref[s] for s in range(nsub)], axis=1).astype(F32)
    o_ref[...] += _dot((w * act).astype(BF16), v_ref[...])


def _peer_ffn(hn, ub, vb, w, h, tm=512, te=1024):
    n = hn.shape[0]
    ne = ub.shape[0]
    return pl.pallas_call(
        _peer_ffn_kernel,
        grid=(n // tm, ne // te),
        in_specs=[pl.BlockSpec((tm, 1024), lambda i, k: (i, 0)),
                  pl.BlockSpec((te, 1024), lambda i, k: (k, 0)),
                  pl.BlockSpec((te, 1024), lambda i, k: (k, 0)),
                  pl.BlockSpec((te // N_KEYS, tm, LANES), lambda i, k: (k, i, 0)),
                  pl.BlockSpec((tm, 1024), lambda i, k: (i, 0))],
        out_specs=pl.BlockSpec((tm, 1024), lambda i, k: (i, 0)),
        out_shape=jax.ShapeDtypeStruct((n, 1024), F32),
        compiler_params=pltpu.CompilerParams(
            dimension_semantics=("arbitrary", "arbitrary"), vmem_limit_bytes=48 << 20),
        name="peer_ffn",
    )(hn, ub, vb, w, h)


def _rope_tables(seq, dim, width, reps):
    r = dim // ROPE_FRACTION
    half = r // 2
    inv_freq = ROPE_THETA ** (-jnp.arange(half, dtype=F32) / half)
    ang = jnp.arange(seq).astype(F32)[:, None] * inv_freq[None, :]
    cos, sin = jnp.cos(ang), jnp.sin(ang)
    rest1 = jnp.ones((seq, width - r), F32)
    rest0 = jnp.zeros((seq, width - r), F32)
    zh = jnp.zeros((seq, half), F32)
    c = jnp.concatenate([cos, cos, rest1], axis=1)
    sa = jnp.concatenate([-sin, zh, rest0], axis=1)
    sb = jnp.concatenate([zh, sin, rest0], axis=1)
    pad1 = jnp.ones((seq, LANES - width * reps), F32)
    pad0 = jnp.zeros((seq, LANES - width * reps), F32)
    c = jnp.concatenate([c] * reps + [pad1], axis=1)
    sa = jnp.concatenate([sa] * reps + [pad0], axis=1)
    sb = jnp.concatenate([sb] * reps + [pad0], axis=1)
    return jnp.stack([c, sa, sb])


def kernel(x, norm1_g, w_in, conv_w, q_norm_g, k_norm_g, w_o, norm2_g,
           peer_wq, peer_k1, peer_k2, peer_u, peer_v):
    batch, seq, d = x.shape
    n = batch * seq
    x2 = x.reshape(n, d)

    splits = [int(c) for c in np.cumsum(COL_SIZES)[:-1]]
    wb, wc, wxt, wq_, wk, wv, wqi, wki, wwi, wga, wgb = jnp.split(w_in, splits, axis=1)
    pad = jnp.zeros((d, LANES - IDX_DIM - IDX_HEADS), w_in.dtype)
    w_in_p = jnp.concatenate([wb, wc, wxt, wq_, wk, wv, wqi, wga, wgb, wki, wwi, pad],
                             axis=1).astype(BF16)

    wq_p = peer_wq.reshape(d, PEER_HEADS, 2, PEER_HALF).transpose(0, 2, 1, 3).reshape(d, 2048).astype(BF16)
    eye = jnp.eye(PEER_HEADS, dtype=peer_k1.dtype)

    def big(kk):
        return jnp.einsum('hnd,hg->nhgd', kk, eye).reshape(N_KEYS * PEER_HEADS, PEER_HEADS * PEER_HALF)

    kbig = jnp.stack([big(peer_k1), big(peer_k2)]).astype(BF16)

    t128 = _rope_tables(seq, HEAD_DIM, HEAD_DIM, 1)
    t64 = _rope_tables(seq, IDX_DIM, IDX_DIM, 2)
    tki = _rope_tables(seq, IDX_DIM, IDX_DIM, 1)

    proj = _in_proj(x2, norm1_g.reshape(1, d), w_in_p)
    qn, kn, vb, qir, kir, wis = _prep(proj, t128, t64, tki, q_norm_g.reshape(1, HEAD_DIM),
                                      k_norm_g.reshape(1, HEAD_DIM), seq)
    yb = _dsa(qn, kn, vb, qir, kir, wis, batch, seq)
    h, hn, pq = _mix(proj, yb, x2, conv_w, w_o.astype(BF16), norm2_g.reshape(1, d), wq_p, seq)
    i1, i2, g = _peer_topk(pq, kbig)
    w = _peer_w(i1, i2, g)
    out = _peer_ffn(hn, peer_u.astype(BF16), peer_v.astype(BF16), w, h)
    return out.reshape(batch, seq, d)
```

```python
import functools
import math

import jax
import jax.numpy as jnp
import numpy as np
from jax import lax
from jax.experimental import pallas as pl
from jax.experimental.pallas import tpu as pltpu

F32 = jnp.float32
BF16 = jnp.bfloat16
I32 = jnp.int32

D_MODEL = 1024
EPS = 1e-6
CONV_K = 3
N_HEADS = 8
HEAD_DIM = 128
N_KV_HEADS = 2
GROUP = N_HEADS // N_KV_HEADS
KV_WIDTH = N_KV_HEADS * HEAD_DIM
IDX_HEADS = 8
IDX_DIM = 64
INDEX_TOPK = 256
ATTN_SCALE = HEAD_DIM ** -0.5
IDX_SCALE = IDX_DIM ** -0.5
IDX_W_SCALE = IDX_HEADS ** -0.5
ROPE_THETA = 500000.0
ROPE_FRACTION = 4
PEER_HEADS = 8
PEER_HALF = 128
N_KEYS = 128
PEER_TOPK = 16

LANES = 128
INT_MIN = -(2 ** 31)
KEY_NEG_INF = INT_MIN + 0x7FFFFF
NEG_INF = float("-inf")

COL_SIZES = (1024, 1024, 1024, 1024, 256, 256, 512, 64, 8, 1024, 1024)
OFF_B, OFF_C, OFF_XT, OFF_Q, OFF_K, OFF_V, OFF_QI, OFF_GA, OFF_GB, OFF_KIWI = (
    0, 1024, 2048, 3072, 4096, 4352, 4608, 5120, 6144, 7168)
PROJ_COLS = 7296
W_SCRATCH_PITCH = 132
W_TOKENS_PER_ITER = 32
DSA_BLOCK = 256
M_FLOOR = -1e30

NT_DIMS = (((1,), (1,)), ((), ()))


def _nt_dot(a, b):
    return lax.dot_general(a, b, NT_DIMS, preferred_element_type=F32)


def _dot(a, b):
    return jnp.dot(a, b, preferred_element_type=F32)


def _in_proj_kernel(x_ref, g_ref, w_ref, o_ref):
    x = x_ref[...]
    xn = x * lax.rsqrt(jnp.mean(x * x, axis=-1, keepdims=True) + EPS) * g_ref[...]
    o_ref[...] = _dot(xn.astype(BF16), w_ref[...])


def _in_proj(x2, g, w, tm=512, tn=2432):
    n = x2.shape[0]
    return pl.pallas_call(
        _in_proj_kernel,
        grid=(PROJ_COLS // tn, n // tm),
        in_specs=[pl.BlockSpec((tm, D_MODEL), lambda j, i: (i, 0)),
                  pl.BlockSpec((1, D_MODEL), lambda j, i: (0, 0)),
                  pl.BlockSpec((D_MODEL, tn), lambda j, i: (0, j))],
        out_specs=pl.BlockSpec((tm, tn), lambda j, i: (i, j)),
        out_shape=jax.ShapeDtypeStruct((n, PROJ_COLS), F32),
        compiler_params=pltpu.CompilerParams(
            dimension_semantics=("arbitrary", "arbitrary"), vmem_limit_bytes=48 << 20),
        name="in_proj",
    )(x2, g, w)


def _rope(x, c, sa, sb, shift):
    left = pltpu.roll(x, LANES - shift, 1)
    right = pltpu.roll(x, shift, 1)
    return x * c + left * sa + right * sb


def _head_norm(x, g):
    return x * lax.rsqrt(jnp.mean(x * x, axis=-1, keepdims=True) + EPS) * g


def _prep_kernel(q_ref, k_ref, v_ref, qi_ref, kiwi_ref, t128_ref, t64_ref, tki_ref,
                 qg_ref, kg_ref,
                 qn_ref, kn_ref, vt_ref, qir_ref, kir_ref, wist_ref):
    c128, sa128, sb128 = t128_ref[0], t128_ref[1], t128_ref[2]
    c64, sa64, sb64 = t64_ref[0], t64_ref[1], t64_ref[2]
    cki, saki, sbki = tki_ref[0], tki_ref[1], tki_ref[2]
    qg = qg_ref[...]
    kg = kg_ref[...]
    for h in range(N_HEADS):
        sl = slice(h * HEAD_DIM, (h + 1) * HEAD_DIM)
        qh = _head_norm(q_ref[:, sl], qg)
        qn_ref[:, sl] = _rope(qh, c128, sa128, sb128, 16).astype(BF16)
    for h in range(N_KV_HEADS):
        sl = slice(h * HEAD_DIM, (h + 1) * HEAD_DIM)
        kh = _head_norm(k_ref[:, sl], kg)
        kn_ref[:, sl] = _rope(kh, c128, sa128, sb128, 16).astype(BF16)
    vt_ref[0] = v_ref[...].T.astype(BF16)
    for p in range(IDX_HEADS // 2):
        sl = slice(p * LANES, (p + 1) * LANES)
        pair = _rope(qi_ref[:, sl], c64, sa64, sb64, 8).astype(BF16)
        qir_ref[0, 2 * p] = pair[:, :IDX_DIM]
        qir_ref[0, 2 * p + 1] = pair[:, IDX_DIM:]
    kiwi = kiwi_ref[...]
    kir_ref[...] = _rope(kiwi, cki, saki, sbki, 8)[:, :IDX_DIM].astype(BF16)
    wist_ref[...] = (kiwi * IDX_W_SCALE * IDX_SCALE).T[IDX_DIM:IDX_DIM + IDX_HEADS, :]


def _prep(proj, t128, t64, tki, qg, kg, seq):
    tm = DSA_BLOCK
    n = proj.shape[0]
    sblk = seq // tm
    tab = lambda i: (0, i % sblk, 0)
    return pl.pallas_call(
        _prep_kernel,
        grid=(n // tm,),
        in_specs=[pl.BlockSpec((tm, 1024), lambda i: (i, OFF_Q // 1024)),
                  pl.BlockSpec((tm, 256), lambda i: (i, OFF_K // 256)),
                  pl.BlockSpec((tm, 256), lambda i: (i, OFF_V // 256)),
                  pl.BlockSpec((tm, 512), lambda i: (i, OFF_QI // 512)),
                  pl.BlockSpec((tm, 128), lambda i: (i, OFF_KIWI // 128)),
                  pl.BlockSpec((3, tm, 128), tab),
                  pl.BlockSpec((3, tm, 128), tab),
                  pl.BlockSpec((3, tm, 128), tab),
                  pl.BlockSpec((1, 128), lambda i: (0, 0)),
                  pl.BlockSpec((1, 128), lambda i: (0, 0))],
        out_specs=[pl.BlockSpec((tm, 1024), lambda i: (i, 0)),
                   pl.BlockSpec((tm, KV_WIDTH), lambda i: (i, 0)),
                   pl.BlockSpec((1, KV_WIDTH, tm), lambda i: (i, 0, 0)),
                   pl.BlockSpec((1, IDX_HEADS, tm, IDX_DIM), lambda i: (i, 0, 0, 0)),
                   pl.BlockSpec((tm, IDX_DIM), lambda i: (i, 0)),
                   pl.BlockSpec((IDX_HEADS, tm), lambda i: (0, i))],
        out_shape=[jax.ShapeDtypeStruct((n, 1024), BF16),
                   jax.ShapeDtypeStruct((n, KV_WIDTH), BF16),
                   jax.ShapeDtypeStruct((n // tm, KV_WIDTH, tm), BF16),
                   jax.ShapeDtypeStruct((n // tm, IDX_HEADS, tm, IDX_DIM), BF16),
                   jax.ShapeDtypeStruct((n, IDX_DIM), BF16),
                   jax.ShapeDtypeStruct((IDX_HEADS, n), F32)],
        compiler_params=pltpu.CompilerParams(dimension_semantics=("arbitrary",)),
        name="prep",
    )(proj, proj, proj, proj, proj, t128, t64, tki, qg, kg)


def _dsa_kernel(qn_ref, kn_ref, vt_ref, qir_ref, kir_ref, wist_ref, o_ref,
                keys_ref, bias_ref, acc_ref, *, k_sel):
    j = pl.program_id(1)
    nch = j + 1
    blk = DSA_BLOCK
    kf = float(k_sel)

    def rowsum(x):
        return jnp.sum(x, axis=0, keepdims=True)

    wist = wist_ref[...]
    qi = qir_ref[0].reshape(IDX_HEADS * blk, IDX_DIM)
    diff = lax.broadcasted_iota(I32, (blk, blk), 0) - lax.broadcasted_iota(I32, (blk, blk), 1)

    def score_chunk(c, carry):
        kc = kir_ref[pl.ds(pl.multiple_of(c * blk, blk), blk), :]
        d = jnp.maximum(_nt_dot(kc, qi), 0.0)
        acc = wist[0:1, :] * d[:, 0:blk]
        for h in range(1, IDX_HEADS):
            acc = acc + wist[h:h + 1, :] * d[:, h * blk:(h + 1) * blk]
        keys_ref[c] = jnp.where(diff <= (j - c) * blk, acc, NEG_INF)
        return carry

    lax.fori_loop(0, nch, score_chunk, 0)

    def count(pred):
        def body(c, cnt):
            m = jnp.where(pred(keys_ref[c]), 1.0, 0.0)
            return cnt + jnp.sum(m.reshape(blk // 8, 8, blk), axis=0)
        return rowsum(lax.fori_loop(0, nch, body, jnp.zeros((8, blk), F32)))

    def as_float(t):
        t = jnp.maximum(t, KEY_NEG_INF)
        return pltpu.bitcast(t ^ ((t >> 31) & 0x7FFFFFFF), F32)

    t0 = jnp.where(count(lambda k: k >= 0.0) >= kf, 0, INT_MIN).astype(I32)

    def bit_body(i, t):
        cand = t | jnp.left_shift(jnp.int32(1), 30 - i)
        cf = as_float(cand)
        return jnp.where(count(lambda k: k >= cf) >= kf, cand, t)

    thr = as_float(lax.fori_loop(0, 31, bit_body, t0))

    need = kf - count(lambda k: k > thr)
    tri = jnp.where(diff >= 0, 1.0, 0.0).astype(BF16)

    def mask_chunk(c, seen):
        key = keys_ref[c]
        eq = jnp.where(key == thr, 1.0, 0.0)
        rank = seen + _dot(tri, eq.astype(BF16))
        tie_bias = jnp.where(rank <= need, 0.0, NEG_INF)
        eq_bias = jnp.where(eq > 0.0, tie_bias, NEG_INF)
        bias = jnp.where(key > thr, 0.0, eq_bias)
        bias_ref[c] = jnp.where(diff <= (j - c) * blk, bias, NEG_INF)
        return seen + rowsum(eq)

    lax.fori_loop(0, nch, mask_chunk, jnp.zeros((1, blk), F32))

    qs = [jnp.concatenate(
        [qn_ref[:, (g * GROUP + n) * HEAD_DIM:(g * GROUP + n + 1) * HEAD_DIM] for n in range(GROUP)],
        axis=0) for g in range(N_KV_HEADS)]
    acc_ref[...] = jnp.zeros(acc_ref.shape, F32)

    def attend(c, carry):
        rows = pl.ds(pl.multiple_of(c * blk, blk), blk)
        bias = jnp.concatenate([bias_ref[c]] * GROUP, axis=1)
        out = []
        for g in range(N_KV_HEADS):
            m, l = carry[g]
            gsl = slice(g * HEAD_DIM, (g + 1) * HEAD_DIM)
            s = _nt_dot(kn_ref[rows, gsl], qs[g]) * ATTN_SCALE + bias
            m_new = jnp.maximum(m, jnp.max(s, axis=0, keepdims=True))
            alpha = jnp.exp(m - m_new)
            p = jnp.exp(s - m_new)
            acc_ref[g] = acc_ref[g] * alpha + _dot(vt_ref[c, gsl, :], p.astype(BF16))
            out.append((m_new, l * alpha + rowsum(p)))
        return tuple(out)

    init = tuple((jnp.full((1, GROUP * blk), M_FLOOR, F32), jnp.zeros((1, GROUP * blk), F32))
                 for _ in range(N_KV_HEADS))
    stats = lax.fori_loop(0, nch, attend, init)
    for g in range(N_KV_HEADS):
        ot = acc_ref[g] / stats[g][1]
        for n in range(GROUP):
            hsl = slice((g * GROUP + n) * HEAD_DIM, (g * GROUP + n + 1) * HEAD_DIM)
            o_ref[:, hsl] = ot[:, n * blk:(n + 1) * blk].T


def _dsa(qn, kn, vt, qir, kir, wist, batch, seq):
    blk = DSA_BLOCK
    n = qn.shape[0]
    nblk = seq // blk
    k_sel = min(INDEX_TOPK, seq // 4)
    qblk = lambda b, j: (b * nblk + j, 0)
    full = lambda b, j: (b, 0)
    return pl.pallas_call(
        functools.partial(_dsa_kernel, k_sel=k_sel),
        grid=(batch, nblk),
        in_specs=[pl.BlockSpec((blk, 1024), qblk),
                  pl.BlockSpec((seq, KV_WIDTH), full),
                  pl.BlockSpec((nblk, KV_WIDTH, blk), lambda b, j: (b, 0, 0)),
                  pl.BlockSpec((1, IDX_HEADS, blk, IDX_DIM), lambda b, j: (b * nblk + j, 0, 0, 0)),
                  pl.BlockSpec((seq, IDX_DIM), full),
                  pl.BlockSpec((IDX_HEADS, blk), lambda b, j: (0, b * nblk + j))],
        out_specs=pl.BlockSpec((blk, 1024), qblk),
        out_shape=jax.ShapeDtypeStruct((n, 1024), F32),
        scratch_shapes=[pltpu.VMEM((nblk, blk, blk), F32),
                        pltpu.VMEM((nblk, blk, blk), F32),
                        pltpu.VMEM((N_KV_HEADS, HEAD_DIM, GROUP * blk), F32)],
        compiler_params=pltpu.CompilerParams(
            dimension_semantics=("arbitrary", "arbitrary"), vmem_limit_bytes=48 << 20),
        name="dsa",
    )(qn, kn, vt, qir, kir, wist)


def _mix_kernel(b_ref, c_ref, xt_ref, ch_ref, xth_ref, ga_ref, gb_ref, yb_ref, x_ref,
                cw_ref, wo_ref, g2_ref, wq_ref,
                h_ref, hn_ref, pq_ref, u_ref, *, blocks_per_seq):
    tm = b_ref.shape[0]
    i = pl.program_id(0)
    first = (i % blocks_per_seq) == 0
    halo = ch_ref[...] * xth_ref[...]
    u_ref[0:8, :] = jnp.where(first, 0.0, halo)
    u = c_ref[...] * xt_ref[...]
    u_ref[8:8 + tm, :] = u
    cw = cw_ref[...]
    conv = cw[0:1, :] * u_ref[6:6 + tm, :] + cw[1:2, :] * u_ref[7:7 + tm, :] + cw[2:3, :] * u
    y_a = b_ref[...] * conv
    merged = jax.nn.sigmoid(ga_ref[...]) * y_a + jax.nn.sigmoid(gb_ref[...]) * yb_ref[...]
    h = x_ref[...] + _dot(merged.astype(BF16), wo_ref[...])
    h_ref[...] = h
    hn = (h * lax.rsqrt(jnp.mean(h * h, axis=-1, keepdims=True) + EPS) * g2_ref[...]).astype(BF16)
    hn_ref[...] = hn
    pq_ref[...] = _dot(hn, wq_ref[...]).astype(BF16)


def _mix(proj, yb, x2, conv_w, wo, g2, wq, seq, tm=256):
    n = x2.shape[0]
    col = lambda off: (lambda i: (i, off // 1024))
    halo = lambda off: (lambda i: (jnp.maximum(i * (tm // 8) - 1, 0), off // 1024))
    row = lambda i: (i, 0)
    const = lambda i: (0, 0)
    return pl.pallas_call(
        functools.partial(_mix_kernel, blocks_per_seq=seq // tm),
        grid=(n // tm,),
        in_specs=[pl.BlockSpec((tm, 1024), col(OFF_B)),
                  pl.BlockSpec((tm, 1024), col(OFF_C)),
                  pl.BlockSpec((tm, 1024), col(OFF_XT)),
                  pl.BlockSpec((8, 1024), halo(OFF_C)),
                  pl.BlockSpec((8, 1024), halo(OFF_XT)),
                  pl.BlockSpec((tm, 1024), col(OFF_GA)),
                  pl.BlockSpec((tm, 1024), col(OFF_GB)),
                  pl.BlockSpec((tm, 1024), row),
                  pl.BlockSpec((tm, 1024), row),
                  pl.BlockSpec((CONV_K, 1024), const),
                  pl.BlockSpec((1024, 1024), const),
                  pl.BlockSpec((1, 1024), const),
                  pl.BlockSpec((1024, 2048), const)],
        out_specs=[pl.BlockSpec((tm, 1024), row),
                   pl.BlockSpec((tm, 1024), row),
                   pl.BlockSpec((tm, 2048), row)],
        out_shape=[jax.ShapeDtypeStruct((n, 1024), F32),
                   jax.ShapeDtypeStruct((n, 1024), BF16),
                   jax.ShapeDtypeStruct((n, 2048), BF16)],
        scratch_shapes=[pltpu.VMEM((tm + 8, 1024), F32)],
        compiler_params=pltpu.CompilerParams(
            dimension_semantics=("arbitrary",), vmem_limit_bytes=48 << 20),
        name="mix",
    )(proj, proj, proj, proj, proj, proj, proj, yb, x2, conv_w, wo, g2, wq)


_CANDS = [(i, j) for i in range(PEER_TOPK) for j in range(PEER_TOPK) if (i + 1) * (j + 1) <= PEER_TOPK]


def _tree(op, xs):
    xs = list(xs)
    while len(xs) > 1:
        xs = [op(xs[i], xs[i + 1]) if i + 1 < len(xs) else xs[i] for i in range(0, len(xs), 2)]
    return xs[0]


def _oddeven_merge(lo, hi, r):
    step = r * 2
    if step < hi - lo:
        yield from _oddeven_merge(lo, hi, step)
        yield from _oddeven_merge(lo + r, hi, step)
        yield from [(i, i + r) for i in range(lo + r, hi - r, step)]
    else:
        yield (lo, lo + r)


def _oddeven_merge_sort(lo, hi):
    if hi - lo >= 1:
        mid = lo + (hi - lo) // 2
        yield from _oddeven_merge_sort(lo, mid)
        yield from _oddeven_merge_sort(mid + 1, hi)
        yield from _oddeven_merge(lo, hi, 1)


_SORT16 = list(_oddeven_merge_sort(0, PEER_TOPK - 1))
_BITONIC16 = [(i, i + d) for d in (8, 4, 2, 1) for i in range(PEER_TOPK) if not i & d]


def _hi(a, b):
    swap = b[0] > a[0]
    return (jnp.maximum(a[0], b[0]),) + tuple(jnp.where(swap, y, x) for x, y in zip(a[1:], b[1:]))


def _ce(a, b):
    swap = b[0] > a[0]
    lo = (jnp.minimum(a[0], b[0]),) + tuple(jnp.where(swap, x, y) for x, y in zip(a[1:], b[1:]))
    return _hi(a, b), lo


def _apply(items, pairs):
    for i, j in pairs:
        items[i], items[j] = _ce(items[i], items[j])
    return items


def _merge_top(a, b):
    m = len(b)
    c = [a[i] if i < PEER_TOPK - m else _hi(a[i], b[PEER_TOPK - 1 - i]) for i in range(PEER_TOPK)]
    return _apply(c, _BITONIC16)


def _peer_topk_kernel(pq_ref, kbig_ref, i1_ref, i2_ref, g_ref,
                      s_ref, val_ref, idx_ref, c_ref, top_ref, n1_ref, n2_ref):
    tm = pq_ref.shape[0]
    hp = PEER_HEADS
    halves = [slice(u * LANES, (u + 1) * LANES) for u in range(tm // LANES)]

    def any_true(flags):
        return jnp.max(_tree(jnp.maximum, flags)) > 0.0

    def tie_flag(top_vals, all_vals, ordered):
        cnt = _tree(jnp.add, [jnp.where(v >= top_vals[PEER_TOPK - 1], 1.0, 0.0) for v in all_vals])
        bad = jnp.where(cnt == float(PEER_TOPK), 0.0, 1.0)
        if ordered:
            bad = _tree(jnp.maximum, [bad] + [jnp.where(top_vals[r] > top_vals[r + 1], 0.0, 1.0)
                                              for r in range(PEER_TOPK - 1)])
        return bad

    for side in range(2):
        q = pq_ref[:, side * 1024:(side + 1) * 1024]
        s_ref[...] = _nt_dot(kbig_ref[side], q).reshape(N_KEYS, hp, tm)

        flags = []
        for ln in halves:
            rows = [s_ref[n, :, ln] for n in range(N_KEYS)]
            groups = [_apply([(rows[g * PEER_TOPK + r], g * PEER_TOPK + r) for r in range(PEER_TOPK)], _SORT16)
                      for g in range(N_KEYS // PEER_TOPK)]
            while len(groups) > 1:
                groups = [_merge_top(groups[u], groups[u + 1]) for u in range(0, len(groups), 2)]
            for r in range(PEER_TOPK):
                val_ref[side, r, :, ln] = groups[0][r][0]
                idx_ref[side, r, :, ln] = groups[0][r][1]
            flags.append(tie_flag([it[0] for it in groups[0]], rows, ordered=True))

        @pl.when(any_true(flags))
        def _():
            def extract(i, carry):
                rows = [s_ref[n] for n in range(N_KEYS)]
                m = _tree(jnp.maximum, rows)
                idx = _tree(jnp.minimum, [jnp.where(rows[n] == m, n, N_KEYS) for n in range(N_KEYS)])
                val_ref[side, i] = m
                idx_ref[side, i] = idx
                for n in range(N_KEYS):
                    s_ref[n] = jnp.where(idx == n, NEG_INF, rows[n])
                return carry

            lax.fori_loop(0, PEER_TOPK, extract, 0)

    flags = []
    for ln in halves:
        def cand(i, j):
            return (val_ref[0, i, :, ln] + val_ref[1, j, :, ln], idx_ref[0, i, :, ln], idx_ref[1, j, :, ln])

        g0 = [cand(0, j) for j in range(16)]
        g1 = _apply([cand(1, j) for j in range(8)] + [cand(i, 0) for i in range(15, 7, -1)], _BITONIC16)
        g2 = _apply([cand(i, j) for i, w in ((2, 5), (3, 4), (4, 3), (5, 2), (6, 2)) for j in range(w)], _SORT16)
        g3 = [cand(7, 0), cand(7, 1)]
        top = _merge_top(_merge_top(g0, g1), _merge_top(g2, g3))
        for k in range(PEER_TOPK):
            top_ref[k, :, ln] = top[k][0]
            n1_ref[k, :, ln] = top[k][1]
            n2_ref[k, :, ln] = top[k][2]
        flags.append(tie_flag([it[0] for it in top], [cand(i, j)[0] for i, j in _CANDS], ordered=False))

    @pl.when(any_true(flags))
    def _():
        for ci, (i, j) in enumerate(_CANDS):
            c_ref[ci] = val_ref[0, i] + val_ref[1, j]
        big = PEER_TOPK * PEER_TOPK

        def pick(k, carry):
            cs = [c_ref[ci] for ci in range(len(_CANDS))]
            m = _tree(jnp.maximum, cs)
            pos = _tree(jnp.minimum, [jnp.where(cs[ci] == m, i * PEER_TOPK + j, big)
                                      for ci, (i, j) in enumerate(_CANDS)])
            top_ref[k] = m
            isel = pos >> 4
            jsel = pos & (PEER_TOPK - 1)
            zero = jnp.zeros((hp, tm), I32)
            n1_ref[k] = _tree(jnp.add, [jnp.where(isel == r, idx_ref[0, r], zero) for r in range(PEER_TOPK)])
            n2_ref[k] = _tree(jnp.add, [jnp.where(jsel == r, idx_ref[1, r], zero) for r in range(PEER_TOPK)])
            for ci, (i, j) in enumerate(_CANDS):
                c_ref[ci] = jnp.where(pos == i * PEER_TOPK + j, NEG_INF, cs[ci])
            return carry

        lax.fori_loop(0, PEER_TOPK, pick, 0)

    top = top_ref[...]
    e = jnp.exp(top - top[0:1])
    gate = e / jnp.sum(e, axis=0, keepdims=True)
    i1_ref[...] = n1_ref[...].reshape(PEER_TOPK * hp, tm).T
    i2_ref[...] = n2_ref[...].reshape(PEER_TOPK * hp, tm).T
    g_ref[...] = gate.reshape(PEER_TOPK * hp, tm).T


def _peer_topk(pq, kbig, tm=256):
    n = pq.shape[0]
    r = PEER_TOPK * PEER_HEADS
    row = lambda i: (i, 0)
    ncand = len(_CANDS)
    return pl.pallas_call(
        _peer_topk_kernel,
        grid=(n // tm,),
        in_specs=[pl.BlockSpec((tm, 2048), row),
                  pl.BlockSpec((2, 1024, 1024), lambda i: (0, 0, 0))],
        out_specs=[pl.BlockSpec((tm, r), row)] * 3,
        out_shape=[jax.ShapeDtypeStruct((n, r), I32),
                   jax.ShapeDtypeStruct((n, r), I32),
                   jax.ShapeDtypeStruct((n, r), F32)],
        scratch_shapes=[pltpu.VMEM((N_KEYS, PEER_HEADS, tm), F32),
                        pltpu.VMEM((2, PEER_TOPK, PEER_HEADS, tm), F32),
                        pltpu.VMEM((2, PEER_TOPK, PEER_HEADS, tm), I32),
                        pltpu.VMEM((ncand, PEER_HEADS, tm), F32),
                        pltpu.VMEM((PEER_TOPK, PEER_HEADS, tm), F32),
                        pltpu.VMEM((PEER_TOPK, PEER_HEADS, tm), I32),
                        pltpu.VMEM((PEER_TOPK, PEER_HEADS, tm), I32)],
        compiler_params=pltpu.CompilerParams(dimension_semantics=("arbitrary",)),
        name="peer_topk",
    )(pq, kbig)


def _peer_w_kernel(i1_ref, i2_ref, g_ref, w_ref, scr_ref):
    tb = i1_ref.shape[0]
    sub = lax.broadcasted_iota(I32, (N_KEYS, LANES), 0)

    def token_group(tg, carry):
        for u in range(W_TOKENS_PER_ITER):
            t = tg * W_TOKENS_PER_ITER + u
            i1 = jnp.broadcast_to(i1_ref[pl.ds(t, 1), :], (N_KEYS, LANES))
            i2 = jnp.broadcast_to(i2_ref[pl.ds(t, 1), :], (N_KEYS, LANES))
            gv = jnp.broadcast_to(g_ref[pl.ds(t, 1), :], (N_KEYS, LANES))
            a = jnp.where(sub == i1, gv, 0.0).astype(BF16)
            bt = jnp.where(sub == i2, 1.0, 0.0).astype(BF16)
            scr_ref[pl.ds(t * W_SCRATCH_PITCH, N_KEYS), :] = _nt_dot(a, bt)
        return carry

    lax.fori_loop(0, tb // W_TOKENS_PER_ITER, token_group, 0)
    for n1 in range(N_KEYS):
        w_ref[n1] = scr_ref[pl.ds(n1, tb, stride=W_SCRATCH_PITCH), :].astype(BF16)


def _peer_w(i1, i2, g, tb=128):
    n = i1.shape[0]
    row = lambda i: (i, 0)
    return pl.pallas_call(
        _peer_w_kernel,
        grid=(n // tb,),
        in_specs=[pl.BlockSpec((tb, 128), row)] * 3,
        out_specs=pl.BlockSpec((N_KEYS, tb, LANES), lambda i: (0, i, 0)),
        out_shape=jax.ShapeDtypeStruct((N_KEYS, n, LANES), BF16),
        scratch_shapes=[pltpu.VMEM((tb * W_SCRATCH_PITCH, LANES), F32)],
        compiler_params=pltpu.CompilerParams(
            dimension_semantics=("arbitrary",), vmem_limit_bytes=48 << 20),
        name="peer_w",
    )(i1, i2, g)


def _peer_ffn_kernel(hn_ref, u_ref, v_ref, w_ref, h_ref, o_ref):
    k = pl.program_id(1)
    nsub = w_ref.shape[0]

    @pl.when(k == 0)
    def _():
        o_ref[...] = h_ref[...]

    pre = _nt_dot(hn_ref[...], u_ref[...])
    act = 0.5 * pre * (1.0 + lax.erf(pre * (1.0 / math.sqrt(2.0))))
    w = jnp.concatenate([w_ref[s] for s in range(nsub)], axis=1).astype(F32)
    o_ref[...] += _dot((w * act).astype(BF16), v_ref[...])


def _peer_ffn(hn, ub, vb, w, h, tm=1024, te=1024):
    n = hn.shape[0]
    ne = ub.shape[0]
    return pl.pallas_call(
        _peer_ffn_kernel,
        grid=(n // tm, ne // te),
        in_specs=[pl.BlockSpec((tm, 1024), lambda i, k: (i, 0)),
                  pl.BlockSpec((te, 1024), lambda i, k: (k, 0)),
                  pl.BlockSpec((te, 1024), lambda i, k: (k, 0)),
                  pl.BlockSpec((te // N_KEYS, tm, LANES), lambda i, k: (k, i, 0)),
                  pl.BlockSpec((tm, 1024), lambda i, k: (i, 0))],
        out_specs=pl.BlockSpec((tm, 1024), lambda i, k: (i, 0)),
        out_shape=jax.ShapeDtypeStruct((n, 1024), F32),
        compiler_params=pltpu.CompilerParams(
            dimension_semantics=("arbitrary", "arbitrary"), vmem_limit_bytes=48 << 20),
        name="peer_ffn",
    )(hn, ub, vb, w, h)


def _rope_tables(seq, dim, width, reps):
    r = dim // ROPE_FRACTION
    half = r // 2
    inv_freq = ROPE_THETA ** (-jnp.arange(half, dtype=F32) / half)
    ang = jnp.arange(seq).astype(F32)[:, None] * inv_freq[None, :]
    cos, sin = jnp.cos(ang), jnp.sin(ang)
    rest1 = jnp.ones((seq, width - r), F32)
    rest0 = jnp.zeros((seq, width - r), F32)
    zh = jnp.zeros((seq, half), F32)
    c = jnp.concatenate([cos, cos, rest1], axis=1)
    sa = jnp.concatenate([-sin, zh, rest0], axis=1)
    sb = jnp.concatenate([zh, sin, rest0], axis=1)
    pad1 = jnp.ones((seq, LANES - width * reps), F32)
    pad0 = jnp.zeros((seq, LANES - width * reps), F32)
    c = jnp.concatenate([c] * reps + [pad1], axis=1)
    sa = jnp.concatenate([sa] * reps + [pad0], axis=1)
    sb = jnp.concatenate([sb] * reps + [pad0], axis=1)
    return jnp.stack([c, sa, sb])


def kernel(x, norm1_g, w_in, conv_w, q_norm_g, k_norm_g, w_o, norm2_g,
           peer_wq, peer_k1, peer_k2, peer_u, peer_v):
    batch, seq, d = x.shape
    n = batch * seq
    assert d == D_MODEL and seq % DSA_BLOCK == 0 and n % 1024 == 0, (batch, seq, d)
    x2 = x.reshape(n, d)

    splits = [int(c) for c in np.cumsum(COL_SIZES)[:-1]]
    wb, wc, wxt, wq_, wk, wv, wqi, wki, wwi, wga, wgb = jnp.split(w_in, splits, axis=1)
    pad = jnp.zeros((d, LANES - IDX_DIM - IDX_HEADS), w_in.dtype)
    w_in_p = jnp.concatenate([wb, wc, wxt, wq_, wk, wv, wqi, wga, wgb, wki, wwi, pad],
                             axis=1).astype(BF16)

    wq_p = peer_wq.reshape(d, PEER_HEADS, 2, PEER_HALF).transpose(0, 2, 1, 3).reshape(d, 2048).astype(BF16)
    eye = jnp.eye(PEER_HEADS, dtype=peer_k1.dtype)

    def big(kk):
        return jnp.einsum('hnd,hg->nhgd', kk, eye).reshape(N_KEYS * PEER_HEADS, PEER_HEADS * PEER_HALF)

    kbig = jnp.stack([big(peer_k1), big(peer_k2)]).astype(BF16)

    t128 = _rope_tables(seq, HEAD_DIM, HEAD_DIM, 1)
    t64 = _rope_tables(seq, IDX_DIM, IDX_DIM, 2)
    tki = _rope_tables(seq, IDX_DIM, IDX_DIM, 1)

    proj = _in_proj(x2, norm1_g.reshape(1, d), w_in_p)
    qn, kn, vb, qir, kir, wis = _prep(proj, t128, t64, tki, q_norm_g.reshape(1, HEAD_DIM),
                                      k_norm_g.reshape(1, HEAD_DIM), seq)
    yb = _dsa(qn, kn, vb, qir, kir, wis, batch, seq)
    h, hn, pq = _mix(proj, yb, x2, conv_w, w_o.astype(BF16), norm2_g.reshape(1, d), wq_p, seq)
    i1, i2, g = _peer_topk(pq, kbig)
    w = _peer_w(i1, i2, g)
    out = _peer_ffn(hn, peer_u.astype(BF16), peer_v.astype(BF16), w, h)
    return out.reshape(batch, seq, d)
```

```python
import functools
import math

import jax
import jax.numpy as jnp
import numpy as np
from jax import lax
from jax.experimental import pallas as pl
from jax.experimental.pallas import tpu as pltpu

F32 = jnp.float32
BF16 = jnp.bfloat16
I32 = jnp.int32

D_MODEL = 1024
EPS = 1e-6
CONV_K = 3
N_HEADS = 8
HEAD_DIM = 128
N_KV_HEADS = 2
GROUP = N_HEADS // N_KV_HEADS
KV_WIDTH = N_KV_HEADS * HEAD_DIM
IDX_HEADS = 8
IDX_DIM = 64
INDEX_TOPK = 256
ATTN_SCALE = HEAD_DIM ** -0.5
IDX_SCALE = IDX_DIM ** -0.5
IDX_W_SCALE = IDX_HEADS ** -0.5
ROPE_THETA = 500000.0
ROPE_FRACTION = 4
PEER_HEADS = 8
PEER_HALF = 128
N_KEYS = 128
PEER_TOPK = 16

LANES = 128
INT_MIN = -(2 ** 31)
KEY_NEG_INF = INT_MIN + 0x7FFFFF
NEG_INF = float("-inf")

COL_SIZES = (1024, 1024, 1024, 1024, 256, 256, 512, 64, 8, 1024, 1024)
OFF_B, OFF_C, OFF_XT, OFF_Q, OFF_K, OFF_V, OFF_QI, OFF_GA, OFF_GB, OFF_KIWI = (
    0, 1024, 2048, 3072, 4096, 4352, 4608, 5120, 6144, 7168)
PROJ_COLS = 7296
W_SCRATCH_PITCH = 132
W_TOKENS_PER_ITER = 32
DSA_BLOCK = 256
M_FLOOR = -1e30

NT_DIMS = (((1,), (1,)), ((), ()))


def _nt_dot(a, b):
    return lax.dot_general(a, b, NT_DIMS, preferred_element_type=F32)


def _dot(a, b):
    return jnp.dot(a, b, preferred_element_type=F32)


def _in_proj_kernel(x_ref, g_ref, w_ref, o_ref):
    x = x_ref[...]
    xn = x * lax.rsqrt(jnp.mean(x * x, axis=-1, keepdims=True) + EPS) * g_ref[...]
    o_ref[...] = _dot(xn.astype(BF16), w_ref[...])


def _in_proj(x2, g, w, tm=512, tn=2432):
    n = x2.shape[0]
    return pl.pallas_call(
        _in_proj_kernel,
        grid=(PROJ_COLS // tn, n // tm),
        in_specs=[pl.BlockSpec((tm, D_MODEL), lambda j, i: (i, 0)),
                  pl.BlockSpec((1, D_MODEL), lambda j, i: (0, 0)),
                  pl.BlockSpec((D_MODEL, tn), lambda j, i: (0, j))],
        out_specs=pl.BlockSpec((tm, tn), lambda j, i: (i, j)),
        out_shape=jax.ShapeDtypeStruct((n, PROJ_COLS), F32),
        compiler_params=pltpu.CompilerParams(
            dimension_semantics=("arbitrary", "arbitrary"), vmem_limit_bytes=48 << 20),
        name="in_proj",
    )(x2, g, w)


def _rope(x, c, sa, sb, shift):
    left = pltpu.roll(x, LANES - shift, 1)
    right = pltpu.roll(x, shift, 1)
    return x * c + left * sa + right * sb


def _head_norm(x, g):
    return x * lax.rsqrt(jnp.mean(x * x, axis=-1, keepdims=True) + EPS) * g


def _prep_kernel(q_ref, k_ref, v_ref, qi_ref, kiwi_ref, t128_ref, t64_ref, tki_ref,
                 qg_ref, kg_ref,
                 qn_ref, kn_ref, vt_ref, qir_ref, kir_ref, wist_ref):
    c128, sa128, sb128 = t128_ref[0], t128_ref[1], t128_ref[2]
    c64, sa64, sb64 = t64_ref[0], t64_ref[1], t64_ref[2]
    cki, saki, sbki = tki_ref[0], tki_ref[1], tki_ref[2]
    qg = qg_ref[...]
    kg = kg_ref[...]
    for h in range(N_HEADS):
        sl = slice(h * HEAD_DIM, (h + 1) * HEAD_DIM)
        qh = _head_norm(q_ref[:, sl], qg)
        qn_ref[:, sl] = _rope(qh, c128, sa128, sb128, 16).astype(BF16)
    for h in range(N_KV_HEADS):
        sl = slice(h * HEAD_DIM, (h + 1) * HEAD_DIM)
        kh = _head_norm(k_ref[:, sl], kg)
        kn_ref[:, sl] = _rope(kh, c128, sa128, sb128, 16).astype(BF16)
    vt_ref[0] = v_ref[...].T.astype(BF16)
    for p in range(IDX_HEADS // 2):
        sl = slice(p * LANES, (p + 1) * LANES)
        pair = _rope(qi_ref[:, sl], c64, sa64, sb64, 8).astype(BF16)
        qir_ref[0, 2 * p] = pair[:, :IDX_DIM]
        qir_ref[0, 2 * p + 1] = pair[:, IDX_DIM:]
    kiwi = kiwi_ref[...]
    kir_ref[...] = _rope(kiwi, cki, saki, sbki, 8)[:, :IDX_DIM].astype(BF16)
    wist_ref[...] = (kiwi * IDX_W_SCALE * IDX_SCALE).T[IDX_DIM:IDX_DIM + IDX_HEADS, :]


def _prep(proj, t128, t64, tki, qg, kg, seq):
    tm = DSA_BLOCK
    n = proj.shape[0]
    sblk = seq // tm
    tab = lambda i: (0, i % sblk, 0)
    return pl.pallas_call(
        _prep_kernel,
        grid=(n // tm,),
        in_specs=[pl.BlockSpec((tm, 1024), lambda i: (i, OFF_Q // 1024)),
                  pl.BlockSpec((tm, 256), lambda i: (i, OFF_K // 256)),
                  pl.BlockSpec((tm, 256), lambda i: (i, OFF_V // 256)),
                  pl.BlockSpec((tm, 512), lambda i: (i, OFF_QI // 512)),
                  pl.BlockSpec((tm, 128), lambda i: (i, OFF_KIWI // 128)),
                  pl.BlockSpec((3, tm, 128), tab),
                  pl.BlockSpec((3, tm, 128), tab),
                  pl.BlockSpec((3, tm, 128), tab),
                  pl.BlockSpec((1, 128), lambda i: (0, 0)),
                  pl.BlockSpec((1, 128), lambda i: (0, 0))],
        out_specs=[pl.BlockSpec((tm, 1024), lambda i: (i, 0)),
                   pl.BlockSpec((tm, KV_WIDTH), lambda i: (i, 0)),
                   pl.BlockSpec((1, KV_WIDTH, tm), lambda i: (i, 0, 0)),
                   pl.BlockSpec((1, IDX_HEADS, tm, IDX_DIM), lambda i: (i, 0, 0, 0)),
                   pl.BlockSpec((tm, IDX_DIM), lambda i: (i, 0)),
                   pl.BlockSpec((IDX_HEADS, tm), lambda i: (0, i))],
        out_shape=[jax.ShapeDtypeStruct((n, 1024), BF16),
                   jax.ShapeDtypeStruct((n, KV_WIDTH), BF16),
                   jax.ShapeDtypeStruct((n // tm, KV_WIDTH, tm), BF16),
                   jax.ShapeDtypeStruct((n // tm, IDX_HEADS, tm, IDX_DIM), BF16),
                   jax.ShapeDtypeStruct((n, IDX_DIM), BF16),
                   jax.ShapeDtypeStruct((IDX_HEADS, n), F32)],
        compiler_params=pltpu.CompilerParams(dimension_semantics=("arbitrary",)),
        name="prep",
    )(proj, proj, proj, proj, proj, t128, t64, tki, qg, kg)


def _dsa_kernel(qn_ref, kn_ref, vt_ref, qir_ref, kir_ref, wist_ref, o_ref,
                keys_ref, bias_ref, acc_ref, *, k_sel):
    j = pl.program_id(1)
    nch = j + 1
    blk = DSA_BLOCK
    kf = float(k_sel)

    def rowsum(x):
        return jnp.sum(x, axis=0, keepdims=True)

    wist = wist_ref[...]
    qi = qir_ref[0].reshape(IDX_HEADS * blk, IDX_DIM)
    diff = lax.broadcasted_iota(I32, (blk, blk), 0) - lax.broadcasted_iota(I32, (blk, blk), 1)

    def score_chunk(c, carry):
        kc = kir_ref[pl.ds(pl.multiple_of(c * blk, blk), blk), :]
        d = jnp.maximum(_nt_dot(kc, qi), 0.0)
        acc = wist[0:1, :] * d[:, 0:blk]
        for h in range(1, IDX_HEADS):
            acc = acc + wist[h:h + 1, :] * d[:, h * blk:(h + 1) * blk]
        keys_ref[c] = jnp.where(diff <= (j - c) * blk, acc, NEG_INF)
        return carry

    lax.fori_loop(0, nch, score_chunk, 0)

    def count(pred):
        def body(c, cnt):
            m = jnp.where(pred(keys_ref[c]), 1.0, 0.0)
            return cnt + jnp.sum(m.reshape(blk // 8, 8, blk), axis=0)
        return rowsum(lax.fori_loop(0, nch, body, jnp.zeros((8, blk), F32)))

    def as_float(t):
        t = jnp.maximum(t, KEY_NEG_INF)
        return pltpu.bitcast(t ^ ((t >> 31) & 0x7FFFFFFF), F32)

    c0 = count(lambda k: k >= 0.0)
    t0 = jnp.where(c0 >= kf, 0, INT_MIN).astype(I32)
    n0 = jnp.where(c0 >= kf, c0, (nch * blk).astype(F32))

    def bit_body(i, carry):
        t, n_ge = carry
        cand = t | jnp.left_shift(jnp.int32(1), 30 - i)
        cf = as_float(cand)
        cnt = count(lambda k: k >= cf)
        return jnp.where(cnt >= kf, cand, t), jnp.where(cnt >= kf, cnt, n_ge)

    thr_key, n_ge = lax.fori_loop(0, 31, bit_body, (t0, n0))
    thr = as_float(thr_key)
    ties = jnp.max(n_ge) > kf

    @pl.when(jnp.logical_not(ties))
    def _():
        def mask_chunk(c, carry):
            bias = jnp.where(keys_ref[c] >= thr, 0.0, NEG_INF)
            bias_ref[c] = jnp.where(diff <= (j - c) * blk, bias, NEG_INF)
            return carry

        lax.fori_loop(0, nch, mask_chunk, 0)

    @pl.when(ties)
    def _():
        need = kf - count(lambda k: k > thr)
        tri = jnp.where(diff >= 0, 1.0, 0.0).astype(BF16)

        def mask_chunk(c, seen):
            key = keys_ref[c]
            eq = jnp.where(key == thr, 1.0, 0.0)
            rank = seen + _dot(tri, eq.astype(BF16))
            tie_bias = jnp.where(rank <= need, 0.0, NEG_INF)
            eq_bias = jnp.where(eq > 0.0, tie_bias, NEG_INF)
            bias = jnp.where(key > thr, 0.0, eq_bias)
            bias_ref[c] = jnp.where(diff <= (j - c) * blk, bias, NEG_INF)
            return seen + rowsum(eq)

        lax.fori_loop(0, nch, mask_chunk, jnp.zeros((1, blk), F32))

    qs = [jnp.concatenate(
        [qn_ref[:, (g * GROUP + n) * HEAD_DIM:(g * GROUP + n + 1) * HEAD_DIM] for n in range(GROUP)],
        axis=0) for g in range(N_KV_HEADS)]
    acc_ref[...] = jnp.zeros(acc_ref.shape, F32)

    def attend(c, carry):
        rows = pl.ds(pl.multiple_of(c * blk, blk), blk)
        bias = jnp.concatenate([bias_ref[c]] * GROUP, axis=1)
        out = []
        for g in range(N_KV_HEADS):
            m, l = carry[g]
            gsl = slice(g * HEAD_DIM, (g + 1) * HEAD_DIM)
            s = _nt_dot(kn_ref[rows, gsl], qs[g]) * ATTN_SCALE + bias
            m_new = jnp.maximum(m, jnp.max(s, axis=0, keepdims=True))
            alpha = jnp.exp(m - m_new)
            p = jnp.exp(s - m_new)
            acc_ref[g] = acc_ref[g] * alpha + _dot(vt_ref[c, gsl, :], p.astype(BF16))
            out.append((m_new, l * alpha + rowsum(p)))
        return tuple(out)

    init = tuple((jnp.full((1, GROUP * blk), M_FLOOR, F32), jnp.zeros((1, GROUP * blk), F32))
                 for _ in range(N_KV_HEADS))
    stats = lax.fori_loop(0, nch, attend, init)
    for g in range(N_KV_HEADS):
        ot = acc_ref[g] / stats[g][1]
        for n in range(GROUP):
            hsl = slice((g * GROUP + n) * HEAD_DIM, (g * GROUP + n + 1) * HEAD_DIM)
            o_ref[:, hsl] = ot[:, n * blk:(n + 1) * blk].T


def _dsa(qn, kn, vt, qir, kir, wist, batch, seq):
    blk = DSA_BLOCK
    n = qn.shape[0]
    nblk = seq // blk
    k_sel = min(INDEX_TOPK, seq // 4)
    qblk = lambda b, j: (b * nblk + j, 0)
    full = lambda b, j: (b, 0)
    return pl.pallas_call(
        functools.partial(_dsa_kernel, k_sel=k_sel),
        grid=(batch, nblk),
        in_specs=[pl.BlockSpec((blk, 1024), qblk),
                  pl.BlockSpec((seq, KV_WIDTH), full),
                  pl.BlockSpec((nblk, KV_WIDTH, blk), lambda b, j: (b, 0, 0)),
                  pl.BlockSpec((1, IDX_HEADS, blk, IDX_DIM), lambda b, j: (b * nblk + j, 0, 0, 0)),
                  pl.BlockSpec((seq, IDX_DIM), full),
                  pl.BlockSpec((IDX_HEADS, blk), lambda b, j: (0, b * nblk + j))],
        out_specs=pl.BlockSpec((blk, 1024), qblk),
        out_shape=jax.ShapeDtypeStruct((n, 1024), F32),
        scratch_shapes=[pltpu.VMEM((nblk, blk, blk), F32),
                        pltpu.VMEM((nblk, blk, blk), F32),
                        pltpu.VMEM((N_KV_HEADS, HEAD_DIM, GROUP * blk), F32)],
        compiler_params=pltpu.CompilerParams(
            dimension_semantics=("arbitrary", "arbitrary"), vmem_limit_bytes=48 << 20),
        name="dsa",
    )(qn, kn, vt, qir, kir, wist)


def _mix_kernel(b_ref, c_ref, xt_ref, ch_ref, xth_ref, ga_ref, gb_ref, yb_ref, x_ref,
                cw_ref, wo_ref, g2_ref, wq_ref,
                h_ref, hn_ref, pq_ref, u_ref, *, blocks_per_seq):
    tm = b_ref.shape[0]
    i = pl.program_id(0)
    first = (i % blocks_per_seq) == 0
    halo = ch_ref[...] * xth_ref[...]
    u_ref[0:8, :] = jnp.where(first, 0.0, halo)
    u = c_ref[...] * xt_ref[...]
    u_ref[8:8 + tm, :] = u
    cw = cw_ref[...]
    conv = cw[0:1, :] * u_ref[6:6 + tm, :] + cw[1:2, :] * u_ref[7:7 + tm, :] + cw[2:3, :] * u
    y_a = b_ref[...] * conv
    merged = jax.nn.sigmoid(ga_ref[...]) * y_a + jax.nn.sigmoid(gb_ref[...]) * yb_ref[...]
    h = x_ref[...] + _dot(merged.astype(BF16), wo_ref[...])
    h_ref[...] = h
    hn = (h * lax.rsqrt(jnp.mean(h * h, axis=-1, keepdims=True) + EPS) * g2_ref[...]).astype(BF16)
    hn_ref[...] = hn
    pq_ref[...] = _dot(hn, wq_ref[...]).astype(BF16)


def _mix(proj, yb, x2, conv_w, wo, g2, wq, seq, tm=256):
    n = x2.shape[0]
    col = lambda off: (lambda i: (i, off // 1024))
    halo = lambda off: (lambda i: (jnp.maximum(i * (tm // 8) - 1, 0), off // 1024))
    row = lambda i: (i, 0)
    const = lambda i: (0, 0)
    return pl.pallas_call(
        functools.partial(_mix_kernel, blocks_per_seq=seq // tm),
        grid=(n // tm,),
        in_specs=[pl.BlockSpec((tm, 1024), col(OFF_B)),
                  pl.BlockSpec((tm, 1024), col(OFF_C)),
                  pl.BlockSpec((tm, 1024), col(OFF_XT)),
                  pl.BlockSpec((8, 1024), halo(OFF_C)),
                  pl.BlockSpec((8, 1024), halo(OFF_XT)),
                  pl.BlockSpec((tm, 1024), col(OFF_GA)),
                  pl.BlockSpec((tm, 1024), col(OFF_GB)),
                  pl.BlockSpec((tm, 1024), row),
                  pl.BlockSpec((tm, 1024), row),
                  pl.BlockSpec((CONV_K, 1024), const),
                  pl.BlockSpec((1024, 1024), const),
                  pl.BlockSpec((1, 1024), const),
                  pl.BlockSpec((1024, 2048), const)],
        out_specs=[pl.BlockSpec((tm, 1024), row),
                   pl.BlockSpec((tm, 1024), row),
                   pl.BlockSpec((tm, 2048), row)],
        out_shape=[jax.ShapeDtypeStruct((n, 1024), F32),
                   jax.ShapeDtypeStruct((n, 1024), BF16),
                   jax.ShapeDtypeStruct((n, 2048), BF16)],
        scratch_shapes=[pltpu.VMEM((tm + 8, 1024), F32)],
        compiler_params=pltpu.CompilerParams(
            dimension_semantics=("arbitrary",), vmem_limit_bytes=48 << 20),
        name="mix",
    )(proj, proj, proj, proj, proj, proj, proj, yb, x2, conv_w, wo, g2, wq)


_CANDS = [(i, j) for i in range(PEER_TOPK) for j in range(PEER_TOPK) if (i + 1) * (j + 1) <= PEER_TOPK]


def _tree(op, xs):
    xs = list(xs)
    while len(xs) > 1:
        xs = [op(xs[i], xs[i + 1]) if i + 1 < len(xs) else xs[i] for i in range(0, len(xs), 2)]
    return xs[0]


def _oddeven_merge(lo, hi, r):
    step = r * 2
    if step < hi - lo:
        yield from _oddeven_merge(lo, hi, step)
        yield from _oddeven_merge(lo + r, hi, step)
        yield from [(i, i + r) for i in range(lo + r, hi - r, step)]
    else:
        yield (lo, lo + r)


def _oddeven_merge_sort(lo, hi):
    if hi - lo >= 1:
        mid = lo + (hi - lo) // 2
        yield from _oddeven_merge_sort(lo, mid)
        yield from _oddeven_merge_sort(mid + 1, hi)
        yield from _oddeven_merge(lo, hi, 1)


_SORT16 = list(_oddeven_merge_sort(0, PEER_TOPK - 1))
_BITONIC16 = [(i, i + d) for d in (8, 4, 2, 1) for i in range(PEER_TOPK) if not i & d]


def _hi(a, b):
    swap = b[0] > a[0]
    return (jnp.maximum(a[0], b[0]),) + tuple(jnp.where(swap, y, x) for x, y in zip(a[1:], b[1:]))


def _ce(a, b):
    swap = b[0] > a[0]
    lo = (jnp.minimum(a[0], b[0]),) + tuple(jnp.where(swap, x, y) for x, y in zip(a[1:], b[1:]))
    return _hi(a, b), lo


def _apply(items, pairs):
    for i, j in pairs:
        items[i], items[j] = _ce(items[i], items[j])
    return items


def _merge_top(a, b):
    m = len(b)
    c = [a[i] if i < PEER_TOPK - m else _hi(a[i], b[PEER_TOPK - 1 - i]) for i in range(PEER_TOPK)]
    return _apply(c, _BITONIC16)


def _peer_topk_kernel(pq_ref, kbig_ref, i1_ref, i2_ref, g_ref,
                      s_ref, val_ref, idx_ref, c_ref, top_ref, n1_ref, n2_ref):
    tm = pq_ref.shape[0]
    hp = PEER_HEADS
    halves = [slice(u * LANES, (u + 1) * LANES) for u in range(tm // LANES)]

    def any_true(flags):
        return jnp.max(_tree(jnp.maximum, flags)) > 0.0

    def tie_flag(top_vals, all_vals, ordered):
        cnt = _tree(jnp.add, [jnp.where(v >= top_vals[PEER_TOPK - 1], 1.0, 0.0) for v in all_vals])
        bad = jnp.where(cnt == float(PEER_TOPK), 0.0, 1.0)
        if ordered:
            bad = _tree(jnp.maximum, [bad] + [jnp.where(top_vals[r] > top_vals[r + 1], 0.0, 1.0)
                                              for r in range(PEER_TOPK - 1)])
        return bad

    for side in range(2):
        q = pq_ref[:, side * 1024:(side + 1) * 1024]
        s_ref[...] = _nt_dot(kbig_ref[side], q).reshape(N_KEYS, hp, tm)

        flags = []
        for ln in halves:
            rows = [s_ref[n, :, ln] for n in range(N_KEYS)]
            groups = [_apply([(rows[g * PEER_TOPK + r], g * PEER_TOPK + r) for r in range(PEER_TOPK)], _SORT16)
                      for g in range(N_KEYS // PEER_TOPK)]
            while len(groups) > 1:
                groups = [_merge_top(groups[u], groups[u + 1]) for u in range(0, len(groups), 2)]
            for r in range(PEER_TOPK):
                val_ref[side, r, :, ln] = groups[0][r][0]
                idx_ref[side, r, :, ln] = groups[0][r][1]
            flags.append(tie_flag([it[0] for it in groups[0]], rows, ordered=True))

        @pl.when(any_true(flags))
        def _():
            def extract(i, carry):
                rows = [s_ref[n] for n in range(N_KEYS)]
                m = _tree(jnp.maximum, rows)
                idx = _tree(jnp.minimum, [jnp.where(rows[n] == m, n, N_KEYS) for n in range(N_KEYS)])
                val_ref[side, i] = m
                idx_ref[side, i] = idx
                for n in range(N_KEYS):
                    s_ref[n] = jnp.where(idx == n, NEG_INF, rows[n])
                return carry

            lax.fori_loop(0, PEER_TOPK, extract, 0)

    flags = []
    for ln in halves:
        def cand(i, j):
            return (val_ref[0, i, :, ln] + val_ref[1, j, :, ln], idx_ref[0, i, :, ln], idx_ref[1, j, :, ln])

        g0 = [cand(0, j) for j in range(16)]
        g1 = _apply([cand(1, j) for j in range(8)] + [cand(i, 0) for i in range(15, 7, -1)], _BITONIC16)
        g2 = _apply([cand(i, j) for i, w in ((2, 5), (3, 4), (4, 3), (5, 2), (6, 2)) for j in range(w)], _SORT16)
        g3 = [cand(7, 0), cand(7, 1)]
        top = _merge_top(_merge_top(g0, g1), _merge_top(g2, g3))
        for k in range(PEER_TOPK):
            top_ref[k, :, ln] = top[k][0]
            n1_ref[k, :, ln] = top[k][1]
            n2_ref[k, :, ln] = top[k][2]
        flags.append(tie_flag([it[0] for it in top], [cand(i, j)[0] for i, j in _CANDS], ordered=False))

    @pl.when(any_true(flags))
    def _():
        for ci, (i, j) in enumerate(_CANDS):
            c_ref[ci] = val_ref[0, i] + val_ref[1, j]
        big = PEER_TOPK * PEER_TOPK

        def pick(k, carry):
            cs = [c_ref[ci] for ci in range(len(_CANDS))]
            m = _tree(jnp.maximum, cs)
            pos = _tree(jnp.minimum, [jnp.where(cs[ci] == m, i * PEER_TOPK + j, big)
                                      for ci, (i, j) in enumerate(_CANDS)])
            top_ref[k] = m
            isel = pos >> 4
            jsel = pos & (PEER_TOPK - 1)
            zero = jnp.zeros((hp, tm), I32)
            n1_ref[k] = _tree(jnp.add, [jnp.where(isel == r, idx_ref[0, r], zero) for r in range(PEER_TOPK)])
            n2_ref[k] = _tree(jnp.add, [jnp.where(jsel == r, idx_ref[1, r], zero) for r in range(PEER_TOPK)])
            for ci, (i, j) in enumerate(_CANDS):
                c_ref[ci] = jnp.where(pos == i * PEER_TOPK + j, NEG_INF, cs[ci])
            return carry

        lax.fori_loop(0, PEER_TOPK, pick, 0)

    top = top_ref[...]
    e = jnp.exp(top - top[0:1])
    gate = e / jnp.sum(e, axis=0, keepdims=True)
    i1_ref[...] = n1_ref[...].reshape(PEER_TOPK * hp, tm).T
    i2_ref[...] = n2_ref[...].reshape(PEER_TOPK * hp, tm).T
    g_ref[...] = gate.reshape(PEER_TOPK * hp, tm).T


def _peer_topk(pq, kbig, tm=256):
    n = pq.shape[0]
    r = PEER_TOPK * PEER_HEADS
    row = lambda i: (i, 0)
    ncand = len(_CANDS)
    return pl.pallas_call(
        _peer_topk_kernel,
        grid=(n // tm,),
        in_specs=[pl.BlockSpec((tm, 2048), row),
                  pl.BlockSpec((2, 1024, 1024), lambda i: (0, 0, 0))],
        out_specs=[pl.BlockSpec((tm, r), row)] * 3,
        out_shape=[jax.ShapeDtypeStruct((n, r), I32),
                   jax.ShapeDtypeStruct((n, r), I32),
                   jax.ShapeDtypeStruct((n, r), F32)],
        scratch_shapes=[pltpu.VMEM((N_KEYS, PEER_HEADS, tm), F32),
                        pltpu.VMEM((2, PEER_TOPK, PEER_HEADS, tm), F32),
                        pltpu.VMEM((2, PEER_TOPK, PEER_HEADS, tm), I32),
                        pltpu.VMEM((ncand, PEER_HEADS, tm), F32),
                        pltpu.VMEM((PEER_TOPK, PEER_HEADS, tm), F32),
                        pltpu.VMEM((PEER_TOPK, PEER_HEADS, tm), I32),
                        pltpu.VMEM((PEER_TOPK, PEER_HEADS, tm), I32)],
        compiler_params=pltpu.CompilerParams(dimension_semantics=("arbitrary",)),
        name="peer_topk",
    )(pq, kbig)


def _peer_w_kernel(i1_ref, i2_ref, g_ref, w_ref, scr_ref):
    tb = i1_ref.shape[0]
    sub = lax.broadcasted_iota(I32, (N_KEYS, LANES), 0)

    def token_group(tg, carry):
        for u in range(W_TOKENS_PER_ITER):
            t = tg * W_TOKENS_PER_ITER + u
            i1 = jnp.broadcast_to(i1_ref[pl.ds(t, 1), :], (N_KEYS, LANES))
            i2 = jnp.broadcast_to(i2_ref[pl.ds(t, 1), :], (N_KEYS, LANES))
            gv = jnp.broadcast_to(g_ref[pl.ds(t, 1), :], (N_KEYS, LANES))
            a = jnp.where(sub == i1, gv, 0.0).astype(BF16)
            bt = jnp.where(sub == i2, 1.0, 0.0).astype(BF16)
            scr_ref[pl.ds(t * W_SCRATCH_PITCH, N_KEYS), :] = _nt_dot(a, bt)
        return carry

    lax.fori_loop(0, tb // W_TOKENS_PER_ITER, token_group, 0)
    for n1 in range(N_KEYS):
        w_ref[n1] = scr_ref[pl.ds(n1, tb, stride=W_SCRATCH_PITCH), :].astype(BF16)


def _peer_w(i1, i2, g, tb=256):
    n = i1.shape[0]
    row = lambda i: (i, 0)
    return pl.pallas_call(
        _peer_w_kernel,
        grid=(n // tb,),
        in_specs=[pl.BlockSpec((tb, 128), row)] * 3,
        out_specs=pl.BlockSpec((N_KEYS, tb, LANES), lambda i: (0, i, 0)),
        out_shape=jax.ShapeDtypeStruct((N_KEYS, n, LANES), BF16),
        scratch_shapes=[pltpu.VMEM((tb * W_SCRATCH_PITCH, LANES), F32)],
        compiler_params=pltpu.CompilerParams(
            dimension_semantics=("arbitrary",), vmem_limit_bytes=48 << 20),
        name="peer_w",
    )(i1, i2, g)


def _peer_ffn_kernel(hn_ref, u_ref, v_ref, w_ref, h_ref, o_ref):
    k = pl.program_id(1)
    nsub = w_ref.shape[0]

    @pl.when(k == 0)
    def _():
        o_ref[...] = h_ref[...]

    pre = _nt_dot(hn_ref[...], u_ref[...])
    act = 0.5 * pre * (1.0 + lax.erf(pre * (1.0 / math.sqrt(2.0))))
    w = jnp.concatenate([w_ref[s] for s in range(nsub)], axis=1).astype(F32)
    o_ref[...] += _dot((w * act).astype(BF16), v_ref[...])


def _peer_ffn(hn, ub, vb, w, h, tm=512, te=2048):
    n = hn.shape[0]
    ne = ub.shape[0]
    return pl.pallas_call(
        _peer_ffn_kernel,
        grid=(n // tm, ne // te),
        in_specs=[pl.BlockSpec((tm, 1024), lambda i, k: (i, 0)),
                  pl.BlockSpec((te, 1024), lambda i, k: (k, 0)),
                  pl.BlockSpec((te, 1024), lambda i, k: (k, 0)),
                  pl.BlockSpec((te // N_KEYS, tm, LANES), lambda i, k: (k, i, 0)),
                  pl.BlockSpec((tm, 1024), lambda i, k: (i, 0))],
        out_specs=pl.BlockSpec((tm, 1024), lambda i, k: (i, 0)),
        out_shape=jax.ShapeDtypeStruct((n, 1024), F32),
        compiler_params=pltpu.CompilerParams(
            dimension_semantics=("arbitrary", "arbitrary"), vmem_limit_bytes=48 << 20),
        name="peer_ffn",
    )(hn, ub, vb, w, h)


def _rope_tables(seq, dim, width, reps):
    r = dim // ROPE_FRACTION
    half = r // 2
    inv_freq = ROPE_THETA ** (-jnp.arange(half, dtype=F32) / half)
    ang = jnp.arange(seq).astype(F32)[:, None] * inv_freq[None, :]
    cos, sin = jnp.cos(ang), jnp.sin(ang)
    rest1 = jnp.ones((seq, width - r), F32)
    rest0 = jnp.zeros((seq, width - r), F32)
    zh = jnp.zeros((seq, half), F32)
    c = jnp.concatenate([cos, cos, rest1], axis=1)
    sa = jnp.concatenate([-sin, zh, rest0], axis=1)
    sb = jnp.concatenate([zh, sin, rest0], axis=1)
    pad1 = jnp.ones((seq, LANES - width * reps), F32)
    pad0 = jnp.zeros((seq, LANES - width * reps), F32)
    c = jnp.concatenate([c] * reps + [pad1], axis=1)
    sa = jnp.concatenate([sa] * reps + [pad0], axis=1)
    sb = jnp.concatenate([sb] * reps + [pad0], axis=1)
    return jnp.stack([c, sa, sb])


def kernel(x, norm1_g, w_in, conv_w, q_norm_g, k_norm_g, w_o, norm2_g,
           peer_wq, peer_k1, peer_k2, peer_u, peer_v):
    batch, seq, d = x.shape
    n = batch * seq
    assert d == D_MODEL and seq % DSA_BLOCK == 0 and n % 1024 == 0, (batch, seq, d)
    x2 = x.reshape(n, d)

    splits = [int(c) for c in np.cumsum(COL_SIZES)[:-1]]
    wb, wc, wxt, wq_, wk, wv, wqi, wki, wwi, wga, wgb = jnp.split(w_in, splits, axis=1)
    pad = jnp.zeros((d, LANES - IDX_DIM - IDX_HEADS), w_in.dtype)
    w_in_p = jnp.concatenate([wb, wc, wxt, wq_, wk, wv, wqi, wga, wgb, wki, wwi, pad],
                             axis=1).astype(BF16)

    wq_p = peer_wq.reshape(d, PEER_HEADS, 2, PEER_HALF).transpose(0, 2, 1, 3).reshape(d, 2048).astype(BF16)
    eye = jnp.eye(PEER_HEADS, dtype=peer_k1.dtype)

    def big(kk):
        return jnp.einsum('hnd,hg->nhgd', kk, eye).reshape(N_KEYS * PEER_HEADS, PEER_HEADS * PEER_HALF)

    kbig = jnp.stack([big(peer_k1), big(peer_k2)]).astype(BF16)

    t128 = _rope_tables(seq, HEAD_DIM, HEAD_DIM, 1)
    t64 = _rope_tables(seq, IDX_DIM, IDX_DIM, 2)
    tki = _rope_tables(seq, IDX_DIM, IDX_DIM, 1)

    proj = _in_proj(x2, norm1_g.reshape(1, d), w_in_p)
    qn, kn, vb, qir, kir, wis = _prep(proj, t128, t64, tki, q_norm_g.reshape(1, HEAD_DIM),
                                      k_norm_g.reshape(1, HEAD_DIM), seq)
    yb = _dsa(qn, kn, vb, qir, kir, wis, batch, seq)
    h, hn, pq = _mix(proj, yb, x2, conv_w, w_o.astype(BF16), norm2_g.reshape(1, d), wq_p, seq)
    i1, i2, g = _peer_topk(pq, kbig)
    w = _peer_w(i1, i2, g)
    out = _peer_ffn(hn, peer_u.astype(BF16), peer_v.astype(BF16), w, h)
    return out.reshape(batch, seq, d)
```

```python
import functools
import math

import jax
import jax.numpy as jnp
import numpy as np
from jax import lax
from jax.experimental import pallas as pl
from jax.experimental.pallas import tpu as pltpu

F32 = jnp.float32
BF16 = jnp.bfloat16
I32 = jnp.int32

D_MODEL = 1024
EPS = 1e-6
CONV_K = 3
N_HEADS = 8
HEAD_DIM = 128
N_KV_HEADS = 2
GROUP = N_HEADS // N_KV_HEADS
KV_WIDTH = N_KV_HEADS * HEAD_DIM
IDX_HEADS = 8
IDX_DIM = 64
INDEX_TOPK = 256
ATTN_SCALE = HEAD_DIM ** -0.5
IDX_SCALE = IDX_DIM ** -0.5
IDX_W_SCALE = IDX_HEADS ** -0.5
ROPE_THETA = 500000.0
ROPE_FRACTION = 4
PEER_HEADS = 8
PEER_HALF = 128
N_KEYS = 128
PEER_TOPK = 16

LANES = 128
INT_MIN = -(2 ** 31)
KEY_NEG_INF = INT_MIN + 0x7FFFFF
NEG_INF = float("-inf")

COL_SIZES = (1024, 1024, 1024, 1024, 256, 256, 512, 64, 8, 1024, 1024)
OFF_B, OFF_C, OFF_XT, OFF_Q, OFF_K, OFF_V, OFF_QI, OFF_GA, OFF_GB, OFF_KIWI = (
    0, 1024, 2048, 3072, 4096, 4352, 4608, 5120, 6144, 7168)
PROJ_COLS = 7296
W_SCRATCH_PITCH = 132
W_TOKENS_PER_ITER = 64
DSA_BLOCK = 256
M_FLOOR = -1e30

NT_DIMS = (((1,), (1,)), ((), ()))


def _nt_dot(a, b):
    return lax.dot_general(a, b, NT_DIMS, preferred_element_type=F32)


def _dot(a, b):
    return jnp.dot(a, b, preferred_element_type=F32)


def _in_proj_kernel(x_ref, g_ref, w_ref, o_ref):
    x = x_ref[...]
    xn = x * lax.rsqrt(jnp.mean(x * x, axis=-1, keepdims=True) + EPS) * g_ref[...]
    o_ref[...] = _dot(xn.astype(BF16), w_ref[...])


def _in_proj(x2, g, w, tm=512, tn=2432):
    n = x2.shape[0]
    return pl.pallas_call(
        _in_proj_kernel,
        grid=(PROJ_COLS // tn, n // tm),
        in_specs=[pl.BlockSpec((tm, D_MODEL), lambda j, i: (i, 0)),
                  pl.BlockSpec((1, D_MODEL), lambda j, i: (0, 0)),
                  pl.BlockSpec((D_MODEL, tn), lambda j, i: (0, j))],
        out_specs=pl.BlockSpec((tm, tn), lambda j, i: (i, j)),
        out_shape=jax.ShapeDtypeStruct((n, PROJ_COLS), F32),
        compiler_params=pltpu.CompilerParams(
            dimension_semantics=("arbitrary", "arbitrary"), vmem_limit_bytes=48 << 20),
        name="in_proj",
    )(x2, g, w)


def _rope(x, c, sa, sb, shift):
    left = pltpu.roll(x, LANES - shift, 1)
    right = pltpu.roll(x, shift, 1)
    return x * c + left * sa + right * sb


def _head_norm(x, g):
    return x * lax.rsqrt(jnp.mean(x * x, axis=-1, keepdims=True) + EPS) * g


def _prep_kernel(q_ref, k_ref, v_ref, qi_ref, kiwi_ref, t128_ref, t64_ref, tki_ref,
                 qg_ref, kg_ref,
                 qn_ref, kn_ref, vt_ref, qir_ref, kir_ref, wist_ref):
    c128, sa128, sb128 = t128_ref[0], t128_ref[1], t128_ref[2]
    c64, sa64, sb64 = t64_ref[0], t64_ref[1], t64_ref[2]
    cki, saki, sbki = tki_ref[0], tki_ref[1], tki_ref[2]
    qg = qg_ref[...]
    kg = kg_ref[...]
    for h in range(N_HEADS):
        sl = slice(h * HEAD_DIM, (h + 1) * HEAD_DIM)
        qh = _head_norm(q_ref[:, sl], qg)
        qn_ref[:, sl] = _rope(qh, c128, sa128, sb128, 16).astype(BF16)
    for h in range(N_KV_HEADS):
        sl = slice(h * HEAD_DIM, (h + 1) * HEAD_DIM)
        kh = _head_norm(k_ref[:, sl], kg)
        kn_ref[:, sl] = _rope(kh, c128, sa128, sb128, 16).astype(BF16)
    vt_ref[0] = v_ref[...].T.astype(BF16)
    for p in range(IDX_HEADS // 2):
        sl = slice(p * LANES, (p + 1) * LANES)
        pair = _rope(qi_ref[:, sl], c64, sa64, sb64, 8).astype(BF16)
        qir_ref[0, 2 * p] = pair[:, :IDX_DIM]
        qir_ref[0, 2 * p + 1] = pair[:, IDX_DIM:]
    kiwi = kiwi_ref[...]
    kir_ref[...] = _rope(kiwi, cki, saki, sbki, 8)[:, :IDX_DIM].astype(BF16)
    wist_ref[...] = (kiwi * IDX_W_SCALE * IDX_SCALE).T[IDX_DIM:IDX_DIM + IDX_HEADS, :]


def _prep(proj, t128, t64, tki, qg, kg, seq):
    tm = DSA_BLOCK
    n = proj.shape[0]
    sblk = seq // tm
    tab = lambda i: (0, i % sblk, 0)
    return pl.pallas_call(
        _prep_kernel,
        grid=(n // tm,),
        in_specs=[pl.BlockSpec((tm, 1024), lambda i: (i, OFF_Q // 1024)),
                  pl.BlockSpec((tm, 256), lambda i: (i, OFF_K // 256)),
                  pl.BlockSpec((tm, 256), lambda i: (i, OFF_V // 256)),
                  pl.BlockSpec((tm, 512), lambda i: (i, OFF_QI // 512)),
                  pl.BlockSpec((tm, 128), lambda i: (i, OFF_KIWI // 128)),
                  pl.BlockSpec((3, tm, 128), tab),
                  pl.BlockSpec((3, tm, 128), tab),
                  pl.BlockSpec((3, tm, 128), tab),
                  pl.BlockSpec((1, 128), lambda i: (0, 0)),
                  pl.BlockSpec((1, 128), lambda i: (0, 0))],
        out_specs=[pl.BlockSpec((tm, 1024), lambda i: (i, 0)),
                   pl.BlockSpec((tm, KV_WIDTH), lambda i: (i, 0)),
                   pl.BlockSpec((1, KV_WIDTH, tm), lambda i: (i, 0, 0)),
                   pl.BlockSpec((1, IDX_HEADS, tm, IDX_DIM), lambda i: (i, 0, 0, 0)),
                   pl.BlockSpec((tm, IDX_DIM), lambda i: (i, 0)),
                   pl.BlockSpec((IDX_HEADS, tm), lambda i: (0, i))],
        out_shape=[jax.ShapeDtypeStruct((n, 1024), BF16),
                   jax.ShapeDtypeStruct((n, KV_WIDTH), BF16),
                   jax.ShapeDtypeStruct((n // tm, KV_WIDTH, tm), BF16),
                   jax.ShapeDtypeStruct((n // tm, IDX_HEADS, tm, IDX_DIM), BF16),
                   jax.ShapeDtypeStruct((n, IDX_DIM), BF16),
                   jax.ShapeDtypeStruct((IDX_HEADS, n), F32)],
        compiler_params=pltpu.CompilerParams(dimension_semantics=("arbitrary",)),
        name="prep",
    )(proj, proj, proj, proj, proj, t128, t64, tki, qg, kg)


def _dsa_kernel(qn_ref, kn_ref, vt_ref, qir_ref, kir_ref, wist_ref, o_ref,
                keys_ref, bias_ref, acc_ref, *, k_sel):
    j = pl.program_id(1)
    nch = j + 1
    blk = DSA_BLOCK
    kf = float(k_sel)

    def rowsum(x):
        return jnp.sum(x, axis=0, keepdims=True)

    wist = wist_ref[...]
    qi = qir_ref[0].reshape(IDX_HEADS * blk, IDX_DIM)
    diff = lax.broadcasted_iota(I32, (blk, blk), 0) - lax.broadcasted_iota(I32, (blk, blk), 1)

    def score_chunk(c, carry):
        kc = kir_ref[pl.ds(pl.multiple_of(c * blk, blk), blk), :]
        d = jnp.maximum(_nt_dot(kc, qi), 0.0)
        acc = wist[0:1, :] * d[:, 0:blk]
        for h in range(1, IDX_HEADS):
            acc = acc + wist[h:h + 1, :] * d[:, h * blk:(h + 1) * blk]
        keys_ref[c] = jnp.where(diff <= (j - c) * blk, acc, NEG_INF)
        return carry

    lax.fori_loop(0, nch, score_chunk, 0)

    def count(pred):
        def body(c, cnt):
            m = jnp.where(pred(keys_ref[c]), 1.0, 0.0)
            return cnt + jnp.sum(m.reshape(blk // 8, 8, blk), axis=0)
        return rowsum(lax.fori_loop(0, nch, body, jnp.zeros((8, blk), F32)))

    def as_float(t):
        t = jnp.maximum(t, KEY_NEG_INF)
        return pltpu.bitcast(t ^ ((t >> 31) & 0x7FFFFFFF), F32)

    c0 = count(lambda k: k >= 0.0)
    t0 = jnp.where(c0 >= kf, 0, INT_MIN).astype(I32)
    n0 = jnp.where(c0 >= kf, c0, (nch * blk).astype(F32))

    def bit_body(i, carry):
        t, n_ge = carry
        cand = t | jnp.left_shift(jnp.int32(1), 30 - i)
        cf = as_float(cand)
        cnt = count(lambda k: k >= cf)
        return jnp.where(cnt >= kf, cand, t), jnp.where(cnt >= kf, cnt, n_ge)

    thr_key, n_ge = lax.fori_loop(0, 31, bit_body, (t0, n0))
    thr = as_float(thr_key)
    ties = jnp.max(n_ge) > kf

    @pl.when(jnp.logical_not(ties))
    def _():
        def mask_chunk(c, carry):
            bias = jnp.where(keys_ref[c] >= thr, 0.0, NEG_INF)
            bias_ref[c] = jnp.where(diff <= (j - c) * blk, bias, NEG_INF)
            return carry

        lax.fori_loop(0, nch, mask_chunk, 0)

    @pl.when(ties)
    def _():
        need = kf - count(lambda k: k > thr)
        tri = jnp.where(diff >= 0, 1.0, 0.0).astype(BF16)

        def mask_chunk(c, seen):
            key = keys_ref[c]
            eq = jnp.where(key == thr, 1.0, 0.0)
            rank = seen + _dot(tri, eq.astype(BF16))
            tie_bias = jnp.where(rank <= need, 0.0, NEG_INF)
            eq_bias = jnp.where(eq > 0.0, tie_bias, NEG_INF)
            bias = jnp.where(key > thr, 0.0, eq_bias)
            bias_ref[c] = jnp.where(diff <= (j - c) * blk, bias, NEG_INF)
            return seen + rowsum(eq)

        lax.fori_loop(0, nch, mask_chunk, jnp.zeros((1, blk), F32))

    qs = [jnp.concatenate(
        [qn_ref[:, (g * GROUP + n) * HEAD_DIM:(g * GROUP + n + 1) * HEAD_DIM] for n in range(GROUP)],
        axis=0) for g in range(N_KV_HEADS)]
    acc_ref[...] = jnp.zeros(acc_ref.shape, F32)

    def attend(c, carry):
        rows = pl.ds(pl.multiple_of(c * blk, blk), blk)
        bias = jnp.concatenate([bias_ref[c]] * GROUP, axis=1)
        out = []
        for g in range(N_KV_HEADS):
            m, l = carry[g]
            gsl = slice(g * HEAD_DIM, (g + 1) * HEAD_DIM)
            s = _nt_dot(kn_ref[rows, gsl], qs[g]) * ATTN_SCALE + bias
            m_new = jnp.maximum(m, jnp.max(s, axis=0, keepdims=True))
            alpha = jnp.exp(m - m_new)
            p = jnp.exp(s - m_new)
            acc_ref[g] = acc_ref[g] * alpha + _dot(vt_ref[c, gsl, :], p.astype(BF16))
            out.append((m_new, l * alpha + rowsum(p)))
        return tuple(out)

    init = tuple((jnp.full((1, GROUP * blk), M_FLOOR, F32), jnp.zeros((1, GROUP * blk), F32))
                 for _ in range(N_KV_HEADS))
    stats = lax.fori_loop(0, nch, attend, init)
    for g in range(N_KV_HEADS):
        ot = acc_ref[g] / stats[g][1]
        for n in range(GROUP):
            hsl = slice((g * GROUP + n) * HEAD_DIM, (g * GROUP + n + 1) * HEAD_DIM)
            o_ref[:, hsl] = ot[:, n * blk:(n + 1) * blk].T


def _dsa(qn, kn, vt, qir, kir, wist, batch, seq):
    blk = DSA_BLOCK
    n = qn.shape[0]
    nblk = seq // blk
    k_sel = min(INDEX_TOPK, seq // 4)
    qblk = lambda b, j: (b * nblk + j, 0)
    full = lambda b, j: (b, 0)
    return pl.pallas_call(
        functools.partial(_dsa_kernel, k_sel=k_sel),
        grid=(batch, nblk),
        in_specs=[pl.BlockSpec((blk, 1024), qblk),
                  pl.BlockSpec((seq, KV_WIDTH), full),
                  pl.BlockSpec((nblk, KV_WIDTH, blk), lambda b, j: (b, 0, 0)),
                  pl.BlockSpec((1, IDX_HEADS, blk, IDX_DIM), lambda b, j: (b * nblk + j, 0, 0, 0)),
                  pl.BlockSpec((seq, IDX_DIM), full),
                  pl.BlockSpec((IDX_HEADS, blk), lambda b, j: (0, b * nblk + j))],
        out_specs=pl.BlockSpec((blk, 1024), qblk),
        out_shape=jax.ShapeDtypeStruct((n, 1024), F32),
        scratch_shapes=[pltpu.VMEM((nblk, blk, blk), F32),
                        pltpu.VMEM((nblk, blk, blk), F32),
                        pltpu.VMEM((N_KV_HEADS, HEAD_DIM, GROUP * blk), F32)],
        compiler_params=pltpu.CompilerParams(
            dimension_semantics=("arbitrary", "arbitrary"), vmem_limit_bytes=48 << 20),
        name="dsa",
    )(qn, kn, vt, qir, kir, wist)


def _mix_kernel(b_ref, c_ref, xt_ref, ch_ref, xth_ref, ga_ref, gb_ref, yb_ref, x_ref,
                cw_ref, wo_ref, g2_ref, wq_ref,
                h_ref, hn_ref, pq_ref, u_ref, *, blocks_per_seq):
    tm = b_ref.shape[0]
    i = pl.program_id(0)
    first = (i % blocks_per_seq) == 0
    halo = ch_ref[...] * xth_ref[...]
    u_ref[0:8, :] = jnp.where(first, 0.0, halo)
    u = c_ref[...] * xt_ref[...]
    u_ref[8:8 + tm, :] = u
    cw = cw_ref[...]
    conv = cw[0:1, :] * u_ref[6:6 + tm, :] + cw[1:2, :] * u_ref[7:7 + tm, :] + cw[2:3, :] * u
    y_a = b_ref[...] * conv
    merged = jax.nn.sigmoid(ga_ref[...]) * y_a + jax.nn.sigmoid(gb_ref[...]) * yb_ref[...]
    h = x_ref[...] + _dot(merged.astype(BF16), wo_ref[...])
    h_ref[...] = h
    hn = (h * lax.rsqrt(jnp.mean(h * h, axis=-1, keepdims=True) + EPS) * g2_ref[...]).astype(BF16)
    hn_ref[...] = hn
    pq_ref[...] = _dot(hn, wq_ref[...]).astype(BF16)


def _mix(proj, yb, x2, conv_w, wo, g2, wq, seq, tm=256):
    n = x2.shape[0]
    col = lambda off: (lambda i: (i, off // 1024))
    halo = lambda off: (lambda i: (jnp.maximum(i * (tm // 8) - 1, 0), off // 1024))
    row = lambda i: (i, 0)
    const = lambda i: (0, 0)
    return pl.pallas_call(
        functools.partial(_mix_kernel, blocks_per_seq=seq // tm),
        grid=(n // tm,),
        in_specs=[pl.BlockSpec((tm, 1024), col(OFF_B)),
                  pl.BlockSpec((tm, 1024), col(OFF_C)),
                  pl.BlockSpec((tm, 1024), col(OFF_XT)),
                  pl.BlockSpec((8, 1024), halo(OFF_C)),
                  pl.BlockSpec((8, 1024), halo(OFF_XT)),
                  pl.BlockSpec((tm, 1024), col(OFF_GA)),
                  pl.BlockSpec((tm, 1024), col(OFF_GB)),
                  pl.BlockSpec((tm, 1024), row),
                  pl.BlockSpec((tm, 1024), row),
                  pl.BlockSpec((CONV_K, 1024), const),
                  pl.BlockSpec((1024, 1024), const),
                  pl.BlockSpec((1, 1024), const),
                  pl.BlockSpec((1024, 2048), const)],
        out_specs=[pl.BlockSpec((tm, 1024), row),
                   pl.BlockSpec((tm, 1024), row),
                   pl.BlockSpec((tm, 2048), row)],
        out_shape=[jax.ShapeDtypeStruct((n, 1024), F32),
                   jax.ShapeDtypeStruct((n, 1024), BF16),
                   jax.ShapeDtypeStruct((n, 2048), BF16)],
        scratch_shapes=[pltpu.VMEM((tm + 8, 1024), F32)],
        compiler_params=pltpu.CompilerParams(
            dimension_semantics=("arbitrary",), vmem_limit_bytes=48 << 20),
        name="mix",
    )(proj, proj, proj, proj, proj, proj, proj, yb, x2, conv_w, wo, g2, wq)


_CANDS = [(i, j) for i in range(PEER_TOPK) for j in range(PEER_TOPK) if (i + 1) * (j + 1) <= PEER_TOPK]


def _tree(op, xs):
    xs = list(xs)
    while len(xs) > 1:
        xs = [op(xs[i], xs[i + 1]) if i + 1 < len(xs) else xs[i] for i in range(0, len(xs), 2)]
    return xs[0]


def _oddeven_merge(lo, hi, r):
    step = r * 2
    if step < hi - lo:
        yield from _oddeven_merge(lo, hi, step)
        yield from _oddeven_merge(lo + r, hi, step)
        yield from [(i, i + r) for i in range(lo + r, hi - r, step)]
    else:
        yield (lo, lo + r)


def _oddeven_merge_sort(lo, hi):
    if hi - lo >= 1:
        mid = lo + (hi - lo) // 2
        yield from _oddeven_merge_sort(lo, mid)
        yield from _oddeven_merge_sort(mid + 1, hi)
        yield from _oddeven_merge(lo, hi, 1)


_SORT16 = list(_oddeven_merge_sort(0, PEER_TOPK - 1))
_BITONIC16 = [(i, i + d) for d in (8, 4, 2, 1) for i in range(PEER_TOPK) if not i & d]


def _hi(a, b):
    swap = b[0] > a[0]
    return (jnp.maximum(a[0], b[0]),) + tuple(jnp.where(swap, y, x) for x, y in zip(a[1:], b[1:]))


def _ce(a, b):
    swap = b[0] > a[0]
    lo = (jnp.minimum(a[0], b[0]),) + tuple(jnp.where(swap, x, y) for x, y in zip(a[1:], b[1:]))
    return _hi(a, b), lo


def _apply(items, pairs):
    for i, j in pairs:
        items[i], items[j] = _ce(items[i], items[j])
    return items


def _merge_top(a, b):
    m = len(b)
    c = [a[i] if i < PEER_TOPK - m else _hi(a[i], b[PEER_TOPK - 1 - i]) for i in range(PEER_TOPK)]
    return _apply(c, _BITONIC16)


def _peer_topk_kernel(pq_ref, kbig_ref, i1_ref, i2_ref, g_ref,
                      s_ref, val_ref, idx_ref, c_ref, top_ref, n1_ref, n2_ref):
    tm = pq_ref.shape[0]
    hp = PEER_HEADS
    halves = [slice(u * LANES, (u + 1) * LANES) for u in range(tm // LANES)]

    def any_true(flags):
        return jnp.max(_tree(jnp.maximum, flags)) > 0.0

    def tie_flag(top_vals, all_vals, ordered):
        cnt = _tree(jnp.add, [jnp.where(v >= top_vals[PEER_TOPK - 1], 1.0, 0.0) for v in all_vals])
        bad = jnp.where(cnt == float(PEER_TOPK), 0.0, 1.0)
        if ordered:
            bad = _tree(jnp.maximum, [bad] + [jnp.where(top_vals[r] > top_vals[r + 1], 0.0, 1.0)
                                              for r in range(PEER_TOPK - 1)])
        return bad

    for side in range(2):
        q = pq_ref[:, side * 1024:(side + 1) * 1024]
        s_ref[...] = _nt_dot(kbig_ref[side], q).reshape(N_KEYS, hp, tm)

        flags = []
        for ln in halves:
            rows = [s_ref[n, :, ln] for n in range(N_KEYS)]
            groups = [_apply([(rows[g * PEER_TOPK + r], g * PEER_TOPK + r) for r in range(PEER_TOPK)], _SORT16)
                      for g in range(N_KEYS // PEER_TOPK)]
            while len(groups) > 1:
                groups = [_merge_top(groups[u], groups[u + 1]) for u in range(0, len(groups), 2)]
            for r in range(PEER_TOPK):
                val_ref[side, r, :, ln] = groups[0][r][0]
                idx_ref[side, r, :, ln] = groups[0][r][1]
            flags.append(tie_flag([it[0] for it in groups[0]], rows, ordered=True))

        @pl.when(any_true(flags))
        def _():
            def extract(i, carry):
                rows = [s_ref[n] for n in range(N_KEYS)]
                m = _tree(jnp.maximum, rows)
                idx = _tree(jnp.minimum, [jnp.where(rows[n] == m, n, N_KEYS) for n in range(N_KEYS)])
                val_ref[side, i] = m
                idx_ref[side, i] = idx
                for n in range(N_KEYS):
                    s_ref[n] = jnp.where(idx == n, NEG_INF, rows[n])
                return carry

            lax.fori_loop(0, PEER_TOPK, extract, 0)

    flags = []
    for ln in halves:
        def cand(i, j):
            return (val_ref[0, i, :, ln] + val_ref[1, j, :, ln], idx_ref[0, i, :, ln], idx_ref[1, j, :, ln])

        g0 = [cand(0, j) for j in range(16)]
        g1 = _apply([cand(1, j) for j in range(8)] + [cand(i, 0) for i in range(15, 7, -1)], _BITONIC16)
        g2 = _apply([cand(i, j) for i, w in ((2, 5), (3, 4), (4, 3), (5, 2), (6, 2)) for j in range(w)], _SORT16)
        g3 = [cand(7, 0), cand(7, 1)]
        top = _merge_top(_merge_top(g0, g1), _merge_top(g2, g3))
        for k in range(PEER_TOPK):
            top_ref[k, :, ln] = top[k][0]
            n1_ref[k, :, ln] = top[k][1]
            n2_ref[k, :, ln] = top[k][2]
        flags.append(tie_flag([it[0] for it in top], [cand(i, j)[0] for i, j in _CANDS], ordered=False))

    @pl.when(any_true(flags))
    def _():
        for ci, (i, j) in enumerate(_CANDS):
            c_ref[ci] = val_ref[0, i] + val_ref[1, j]
        big = PEER_TOPK * PEER_TOPK

        def pick(k, carry):
            cs = [c_ref[ci] for ci in range(len(_CANDS))]
            m = _tree(jnp.maximum, cs)
            pos = _tree(jnp.minimum, [jnp.where(cs[ci] == m, i * PEER_TOPK + j, big)
                                      for ci, (i, j) in enumerate(_CANDS)])
            top_ref[k] = m
            isel = pos >> 4
            jsel = pos & (PEER_TOPK - 1)
            zero = jnp.zeros((hp, tm), I32)
            n1_ref[k] = _tree(jnp.add, [jnp.where(isel == r, idx_ref[0, r], zero) for r in range(PEER_TOPK)])
            n2_ref[k] = _tree(jnp.add, [jnp.where(jsel == r, idx_ref[1, r], zero) for r in range(PEER_TOPK)])
            for ci, (i, j) in enumerate(_CANDS):
                c_ref[ci] = jnp.where(pos == i * PEER_TOPK + j, NEG_INF, cs[ci])
            return carry

        lax.fori_loop(0, PEER_TOPK, pick, 0)

    top = top_ref[...]
    e = jnp.exp(top - top[0:1])
    gate = e / jnp.sum(e, axis=0, keepdims=True)
    i1_ref[...] = n1_ref[...].reshape(PEER_TOPK * hp, tm).T
    i2_ref[...] = n2_ref[...].reshape(PEER_TOPK * hp, tm).T
    g_ref[...] = gate.reshape(PEER_TOPK * hp, tm).T


def _peer_topk(pq, kbig, tm=256):
    n = pq.shape[0]
    r = PEER_TOPK * PEER_HEADS
    row = lambda i: (i, 0)
    ncand = len(_CANDS)
    return pl.pallas_call(
        _peer_topk_kernel,
        grid=(n // tm,),
        in_specs=[pl.BlockSpec((tm, 2048), row),
                  pl.BlockSpec((2, 1024, 1024), lambda i: (0, 0, 0))],
        out_specs=[pl.BlockSpec((tm, r), row)] * 3,
        out_shape=[jax.ShapeDtypeStruct((n, r), I32),
                   jax.ShapeDtypeStruct((n, r), I32),
                   jax.ShapeDtypeStruct((n, r), F32)],
        scratch_shapes=[pltpu.VMEM((N_KEYS, PEER_HEADS, tm), F32),
                        pltpu.VMEM((2, PEER_TOPK, PEER_HEADS, tm), F32),
                        pltpu.VMEM((2, PEER_TOPK, PEER_HEADS, tm), I32),
                        pltpu.VMEM((ncand, PEER_HEADS, tm), F32),
                        pltpu.VMEM((PEER_TOPK, PEER_HEADS, tm), F32),
                        pltpu.VMEM((PEER_TOPK, PEER_HEADS, tm), I32),
                        pltpu.VMEM((PEER_TOPK, PEER_HEADS, tm), I32)],
        compiler_params=pltpu.CompilerParams(dimension_semantics=("arbitrary",)),
        name="peer_topk",
    )(pq, kbig)


def _peer_w_kernel(i1_ref, i2_ref, g_ref, w_ref, scr_ref):
    tb = i1_ref.shape[0]
    sub = lax.broadcasted_iota(I32, (N_KEYS, LANES), 0)

    def token_group(tg, carry):
        for u in range(W_TOKENS_PER_ITER):
            t = tg * W_TOKENS_PER_ITER + u
            i1 = jnp.broadcast_to(i1_ref[pl.ds(t, 1), :], (N_KEYS, LANES))
            i2 = jnp.broadcast_to(i2_ref[pl.ds(t, 1), :], (N_KEYS, LANES))
            gv = jnp.broadcast_to(g_ref[pl.ds(t, 1), :], (N_KEYS, LANES))
            a = jnp.where(sub == i1, gv, 0.0).astype(BF16)
            bt = jnp.where(sub == i2, 1.0, 0.0).astype(BF16)
            scr_ref[pl.ds(t * W_SCRATCH_PITCH, N_KEYS), :] = _nt_dot(a, bt)
        return carry

    lax.fori_loop(0, tb // W_TOKENS_PER_ITER, token_group, 0)
    for n1 in range(N_KEYS):
        w_ref[n1] = scr_ref[pl.ds(n1, tb, stride=W_SCRATCH_PITCH), :].astype(BF16)


def _peer_w(i1, i2, g, tb=256):
    n = i1.shape[0]
    row = lambda i: (i, 0)
    return pl.pallas_call(
        _peer_w_kernel,
        grid=(n // tb,),
        in_specs=[pl.BlockSpec((tb, 128), row)] * 3,
        out_specs=pl.BlockSpec((N_KEYS, tb, LANES), lambda i: (0, i, 0)),
        out_shape=jax.ShapeDtypeStruct((N_KEYS, n, LANES), BF16),
        scratch_shapes=[pltpu.VMEM((tb * W_SCRATCH_PITCH, LANES), F32)],
        compiler_params=pltpu.CompilerParams(
            dimension_semantics=("arbitrary",), vmem_limit_bytes=48 << 20),
        name="peer_w",
    )(i1, i2, g)


def _peer_ffn_kernel(hn_ref, u_ref, v_ref, w_ref, h_ref, o_ref):
    k = pl.program_id(1)
    nsub = w_ref.shape[0]

    @pl.when(k == 0)
    def _():
        o_ref[...] = h_ref[...]

    pre = _nt_dot(hn_ref[...], u_ref[...])
    act = 0.5 * pre * (1.0 + lax.erf(pre * (1.0 / math.sqrt(2.0))))
    w = jnp.concatenate([w_ref[s] for s in range(nsub)], axis=1).astype(F32)
    o_ref[...] += _dot((w * act).astype(BF16), v_ref[...])


def _peer_ffn(hn, ub, vb, w, h, tm=1024, te=1024):
    n = hn.shape[0]
    ne = ub.shape[0]
    return pl.pallas_call(
        _peer_ffn_kernel,
        grid=(n // tm, ne // te),
        in_specs=[pl.BlockSpec((tm, 1024), lambda i, k: (i, 0)),
                  pl.BlockSpec((te, 1024), lambda i, k: (k, 0)),
                  pl.BlockSpec((te, 1024), lambda i, k: (k, 0)),
                  pl.BlockSpec((te // N_KEYS, tm, LANES), lambda i, k: (k, i, 0)),
                  pl.BlockSpec((tm, 1024), lambda i, k: (i, 0))],
        out_specs=pl.BlockSpec((tm, 1024), lambda i, k: (i, 0)),
        out_shape=jax.ShapeDtypeStruct((n, 1024), F32),
        compiler_params=pltpu.CompilerParams(
            dimension_semantics=("arbitrary", "arbitrary"), vmem_limit_bytes=48 << 20),
        name="peer_ffn",
    )(hn, ub, vb, w, h)


def _rope_tables(seq, dim, width, reps):
    r = dim // ROPE_FRACTION
    half = r // 2
    inv_freq = ROPE_THETA ** (-jnp.arange(half, dtype=F32) / half)
    ang = jnp.arange(seq).astype(F32)[:, None] * inv_freq[None, :]
    cos, sin = jnp.cos(ang), jnp.sin(ang)
    rest1 = jnp.ones((seq, width - r), F32)
    rest0 = jnp.zeros((seq, width - r), F32)
    zh = jnp.zeros((seq, half), F32)
    c = jnp.concatenate([cos, cos, rest1], axis=1)
    sa = jnp.concatenate([-sin, zh, rest0], axis=1)
    sb = jnp.concatenate([zh, sin, rest0], axis=1)
    pad1 = jnp.ones((seq, LANES - width * reps), F32)
    pad0 = jnp.zeros((seq, LANES - width * reps), F32)
    c = jnp.concatenate([c] * reps + [pad1], axis=1)
    sa = jnp.concatenate([sa] * reps + [pad0], axis=1)
    sb = jnp.concatenate([sb] * reps + [pad0], axis=1)
    return jnp.stack([c, sa, sb])


def kernel(x, norm1_g, w_in, conv_w, q_norm_g, k_norm_g, w_o, norm2_g,
           peer_wq, peer_k1, peer_k2, peer_u, peer_v):
    batch, seq, d = x.shape
    n = batch * seq
    assert d == D_MODEL and seq % DSA_BLOCK == 0 and n % 1024 == 0, (batch, seq, d)
    x2 = x.reshape(n, d)

    splits = [int(c) for c in np.cumsum(COL_SIZES)[:-1]]
    wb, wc, wxt, wq_, wk, wv, wqi, wki, wwi, wga, wgb = jnp.split(w_in.astype(BF16), splits, axis=1)
    pad = jnp.zeros((d, LANES - IDX_DIM - IDX_HEADS), BF16)
    w_in_p = jnp.concatenate([wb, wc, wxt, wq_, wk, wv, wqi, wga, wgb, wki, wwi, pad], axis=1)

    wq_p = peer_wq.astype(BF16).reshape(d, PEER_HEADS, 2, PEER_HALF).transpose(0, 2, 1, 3).reshape(d, 2048)
    eye = jnp.eye(PEER_HEADS, dtype=BF16)

    def big(kk):
        return jnp.einsum('hnd,hg->nhgd', kk.astype(BF16), eye).reshape(
            N_KEYS * PEER_HEADS, PEER_HEADS * PEER_HALF)

    kbig = jnp.stack([big(peer_k1), big(peer_k2)])

    t128 = _rope_tables(seq, HEAD_DIM, HEAD_DIM, 1)
    t64 = _rope_tables(seq, IDX_DIM, IDX_DIM, 2)
    tki = _rope_tables(seq, IDX_DIM, IDX_DIM, 1)

    proj = _in_proj(x2, norm1_g.reshape(1, d), w_in_p)
    qn, kn, vb, qir, kir, wis = _prep(proj, t128, t64, tki, q_norm_g.reshape(1, HEAD_DIM),
                                      k_norm_g.reshape(1, HEAD_DIM), seq)
    yb = _dsa(qn, kn, vb, qir, kir, wis, batch, seq)
    h, hn, pq = _mix(proj, yb, x2, conv_w, w_o.astype(BF16), norm2_g.reshape(1, d), wq_p, seq)
    i1, i2, g = _peer_topk(pq, kbig)
    w = _peer_w(i1, i2, g)
    out = _peer_ffn(hn, peer_u.astype(BF16), peer_v.astype(BF16), w, h)
    return out.reshape(batch, seq, d)
```

```python
import functools
import math

import jax
import jax.numpy as jnp
import numpy as np
from jax import lax
from jax.experimental import pallas as pl
from jax.experimental.pallas import tpu as pltpu

F32 = jnp.float32
BF16 = jnp.bfloat16
I32 = jnp.int32

D_MODEL = 1024
EPS = 1e-6
CONV_K = 3
N_HEADS = 8
HEAD_DIM = 128
N_KV_HEADS = 2
GROUP = N_HEADS // N_KV_HEADS
KV_WIDTH = N_KV_HEADS * HEAD_DIM
IDX_HEADS = 8
IDX_DIM = 64
INDEX_TOPK = 256
ATTN_SCALE = HEAD_DIM ** -0.5
IDX_SCALE = IDX_DIM ** -0.5
IDX_W_SCALE = IDX_HEADS ** -0.5
ROPE_THETA = 500000.0
ROPE_FRACTION = 4
PEER_HEADS = 8
PEER_HALF = 128
N_KEYS = 128
PEER_TOPK = 16

LANES = 128
INT_MIN = -(2 ** 31)
KEY_NEG_INF = INT_MIN + 0x7FFFFF
NEG_INF = float("-inf")

COL_SIZES = (1024, 1024, 1024, 1024, 256, 256, 512, 64, 8, 1024, 1024)
OFF_B, OFF_C, OFF_XT, OFF_Q, OFF_K, OFF_V, OFF_QI = 0, 1024, 2048, 3072, 4096, 4352, 4608
MAIN_COLS = 5120
OFF_GA, OFF_GB, OFF_KIWI = 0, 1024, 2048
TAIL_COLS = 2176
W_SCRATCH_PITCH = 132
W_TOKENS_PER_ITER = 64
DSA_BLOCK = 256
M_FLOOR = -1e30

NT_DIMS = (((1,), (1,)), ((), ()))


def _nt_dot(a, b):
    return lax.dot_general(a, b, NT_DIMS, preferred_element_type=F32)


def _dot(a, b):
    return jnp.dot(a, b, preferred_element_type=F32)


def _in_proj_kernel(x_ref, g_ref, w_ref, o_ref):
    x = x_ref[...]
    xn = x * lax.rsqrt(jnp.mean(x * x, axis=-1, keepdims=True) + EPS) * g_ref[...]
    o_ref[...] = _dot(xn.astype(BF16), w_ref[...])


def _in_proj(x2, g, w, cols, tn, name, tm=512):
    n = x2.shape[0]
    return pl.pallas_call(
        _in_proj_kernel,
        grid=(cols // tn, n // tm),
        in_specs=[pl.BlockSpec((tm, D_MODEL), lambda j, i: (i, 0)),
                  pl.BlockSpec((1, D_MODEL), lambda j, i: (0, 0)),
                  pl.BlockSpec((D_MODEL, tn), lambda j, i: (0, j))],
        out_specs=pl.BlockSpec((tm, tn), lambda j, i: (i, j)),
        out_shape=jax.ShapeDtypeStruct((n, cols), F32),
        compiler_params=pltpu.CompilerParams(
            dimension_semantics=("arbitrary", "arbitrary"), vmem_limit_bytes=48 << 20),
        name=name,
    )(x2, g, w)


def _rope(x, c, sa, sb, shift):
    left = pltpu.roll(x, LANES - shift, 1)
    right = pltpu.roll(x, shift, 1)
    return x * c + left * sa + right * sb


def _head_norm(x, g):
    return x * lax.rsqrt(jnp.mean(x * x, axis=-1, keepdims=True) + EPS) * g


def _prep_kernel(q_ref, k_ref, v_ref, qi_ref, kiwi_ref, t128_ref, t64_ref, tki_ref,
                 qg_ref, kg_ref,
                 qn_ref, kn_ref, vt_ref, qir_ref, kir_ref, wist_ref):
    c128, sa128, sb128 = t128_ref[0], t128_ref[1], t128_ref[2]
    c64, sa64, sb64 = t64_ref[0], t64_ref[1], t64_ref[2]
    cki, saki, sbki = tki_ref[0], tki_ref[1], tki_ref[2]
    qg = qg_ref[...]
    kg = kg_ref[...]
    for h in range(N_HEADS):
        sl = slice(h * HEAD_DIM, (h + 1) * HEAD_DIM)
        qh = _head_norm(q_ref[:, sl], qg)
        qn_ref[:, sl] = _rope(qh, c128, sa128, sb128, 16).astype(BF16)
    for h in range(N_KV_HEADS):
        sl = slice(h * HEAD_DIM, (h + 1) * HEAD_DIM)
        kh = _head_norm(k_ref[:, sl], kg)
        kn_ref[:, sl] = _rope(kh, c128, sa128, sb128, 16).astype(BF16)
    vt_ref[0] = v_ref[...].T.astype(BF16)
    for p in range(IDX_HEADS // 2):
        sl = slice(p * LANES, (p + 1) * LANES)
        pair = _rope(qi_ref[:, sl], c64, sa64, sb64, 8).astype(BF16)
        qir_ref[0, 2 * p] = pair[:, :IDX_DIM]
        qir_ref[0, 2 * p + 1] = pair[:, IDX_DIM:]
    kiwi = kiwi_ref[...]
    kir_ref[...] = _rope(kiwi, cki, saki, sbki, 8)[:, :IDX_DIM].astype(BF16)
    wist_ref[...] = (kiwi * IDX_W_SCALE * IDX_SCALE).T[IDX_DIM:IDX_DIM + IDX_HEADS, :]


def _prep(proj, tail, t128, t64, tki, qg, kg, seq):
    tm = DSA_BLOCK
    n = proj.shape[0]
    sblk = seq // tm
    tab = lambda i: (0, i % sblk, 0)
    return pl.pallas_call(
        _prep_kernel,
        grid=(n // tm,),
        in_specs=[pl.BlockSpec((tm, 1024), lambda i: (i, OFF_Q // 1024)),
                  pl.BlockSpec((tm, 256), lambda i: (i, OFF_K // 256)),
                  pl.BlockSpec((tm, 256), lambda i: (i, OFF_V // 256)),
                  pl.BlockSpec((tm, 512), lambda i: (i, OFF_QI // 512)),
                  pl.BlockSpec((tm, 128), lambda i: (i, OFF_KIWI // 128)),
                  pl.BlockSpec((3, tm, 128), tab),
                  pl.BlockSpec((3, tm, 128), tab),
                  pl.BlockSpec((3, tm, 128), tab),
                  pl.BlockSpec((1, 128), lambda i: (0, 0)),
                  pl.BlockSpec((1, 128), lambda i: (0, 0))],
        out_specs=[pl.BlockSpec((tm, 1024), lambda i: (i, 0)),
                   pl.BlockSpec((tm, KV_WIDTH), lambda i: (i, 0)),
                   pl.BlockSpec((1, KV_WIDTH, tm), lambda i: (i, 0, 0)),
                   pl.BlockSpec((1, IDX_HEADS, tm, IDX_DIM), lambda i: (i, 0, 0, 0)),
                   pl.BlockSpec((tm, IDX_DIM), lambda i: (i, 0)),
                   pl.BlockSpec((IDX_HEADS, tm), lambda i: (0, i))],
        out_shape=[jax.ShapeDtypeStruct((n, 1024), BF16),
                   jax.ShapeDtypeStruct((n, KV_WIDTH), BF16),
                   jax.ShapeDtypeStruct((n // tm, KV_WIDTH, tm), BF16),
                   jax.ShapeDtypeStruct((n // tm, IDX_HEADS, tm, IDX_DIM), BF16),
                   jax.ShapeDtypeStruct((n, IDX_DIM), BF16),
                   jax.ShapeDtypeStruct((IDX_HEADS, n), F32)],
        compiler_params=pltpu.CompilerParams(dimension_semantics=("arbitrary",)),
        name="prep",
    )(proj, proj, proj, proj, tail, t128, t64, tki, qg, kg)


def _dsa_kernel(qn_ref, kn_ref, vt_ref, qir_ref, kir_ref, wist_ref, o_ref,
                keys_ref, bias_ref, acc_ref, *, k_sel):
    j = pl.program_id(1)
    nch = j + 1
    blk = DSA_BLOCK
    kf = float(k_sel)

    def rowsum(x):
        return jnp.sum(x, axis=0, keepdims=True)

    wist = wist_ref[...]
    qi = qir_ref[0].reshape(IDX_HEADS * blk, IDX_DIM)
    diff = lax.broadcasted_iota(I32, (blk, blk), 0) - lax.broadcasted_iota(I32, (blk, blk), 1)

    def score_chunk(c, carry):
        kc = kir_ref[pl.ds(pl.multiple_of(c * blk, blk), blk), :]
        d = jnp.maximum(_nt_dot(kc, qi), 0.0)
        acc = wist[0:1, :] * d[:, 0:blk]
        for h in range(1, IDX_HEADS):
            acc = acc + wist[h:h + 1, :] * d[:, h * blk:(h + 1) * blk]
        keys_ref[c] = jnp.where(diff <= (j - c) * blk, acc, NEG_INF)
        return carry

    lax.fori_loop(0, nch, score_chunk, 0)

    def count(pred):
        def body(c, cnt):
            m = jnp.where(pred(keys_ref[c]), 1.0, 0.0)
            return cnt + jnp.sum(m.reshape(blk // 8, 8, blk), axis=0)
        return rowsum(lax.fori_loop(0, nch, body, jnp.zeros((8, blk), F32)))

    def as_float(t):
        t = jnp.maximum(t, KEY_NEG_INF)
        return pltpu.bitcast(t ^ ((t >> 31) & 0x7FFFFFFF), F32)

    c0 = count(lambda k: k >= 0.0)
    t0 = jnp.where(c0 >= kf, 0, INT_MIN).astype(I32)
    n0 = jnp.where(c0 >= kf, c0, (nch * blk).astype(F32))

    def bit_body(i, carry):
        t, n_ge = carry
        cand = t | jnp.left_shift(jnp.int32(1), 30 - i)
        cf = as_float(cand)
        cnt = count(lambda k: k >= cf)
        return jnp.where(cnt >= kf, cand, t), jnp.where(cnt >= kf, cnt, n_ge)

    thr_key, n_ge = lax.fori_loop(0, 31, bit_body, (t0, n0))
    thr = as_float(thr_key)
    ties = jnp.max(n_ge) > kf

    @pl.when(jnp.logical_not(ties))
    def _():
        def mask_chunk(c, carry):
            bias = jnp.where(keys_ref[c] >= thr, 0.0, NEG_INF)
            bias_ref[c] = jnp.where(diff <= (j - c) * blk, bias, NEG_INF)
            return carry

        lax.fori_loop(0, nch, mask_chunk, 0)

    @pl.when(ties)
    def _():
        need = kf - count(lambda k: k > thr)
        tri = jnp.where(diff >= 0, 1.0, 0.0).astype(BF16)

        def mask_chunk(c, seen):
            key = keys_ref[c]
            eq = jnp.where(key == thr, 1.0, 0.0)
            rank = seen + _dot(tri, eq.astype(BF16))
            tie_bias = jnp.where(rank <= need, 0.0, NEG_INF)
            eq_bias = jnp.where(eq > 0.0, tie_bias, NEG_INF)
            bias = jnp.where(key > thr, 0.0, eq_bias)
            bias_ref[c] = jnp.where(diff <= (j - c) * blk, bias, NEG_INF)
            return seen + rowsum(eq)

        lax.fori_loop(0, nch, mask_chunk, jnp.zeros((1, blk), F32))

    qs = [jnp.concatenate(
        [qn_ref[:, (g * GROUP + n) * HEAD_DIM:(g * GROUP + n + 1) * HEAD_DIM] for n in range(GROUP)],
        axis=0) for g in range(N_KV_HEADS)]
    acc_ref[...] = jnp.zeros(acc_ref.shape, F32)

    def attend(c, carry):
        rows = pl.ds(pl.multiple_of(c * blk, blk), blk)
        bias = jnp.concatenate([bias_ref[c]] * GROUP, axis=1)
        out = []
        for g in range(N_KV_HEADS):
            m, l = carry[g]
            gsl = slice(g * HEAD_DIM, (g + 1) * HEAD_DIM)
            s = _nt_dot(kn_ref[rows, gsl], qs[g]) * ATTN_SCALE + bias
            m_new = jnp.maximum(m, jnp.max(s, axis=0, keepdims=True))
            alpha = jnp.exp(m - m_new)
            p = jnp.exp(s - m_new)
            acc_ref[g] = acc_ref[g] * alpha + _dot(vt_ref[c, gsl, :], p.astype(BF16))
            out.append((m_new, l * alpha + rowsum(p)))
        return tuple(out)

    init = tuple((jnp.full((1, GROUP * blk), M_FLOOR, F32), jnp.zeros((1, GROUP * blk), F32))
                 for _ in range(N_KV_HEADS))
    stats = lax.fori_loop(0, nch, attend, init)
    for g in range(N_KV_HEADS):
        ot = acc_ref[g] / stats[g][1]
        for n in range(GROUP):
            hsl = slice((g * GROUP + n) * HEAD_DIM, (g * GROUP + n + 1) * HEAD_DIM)
            o_ref[:, hsl] = ot[:, n * blk:(n + 1) * blk].T


def _dsa(qn, kn, vt, qir, kir, wist, batch, seq):
    blk = DSA_BLOCK
    n = qn.shape[0]
    nblk = seq // blk
    k_sel = min(INDEX_TOPK, seq // 4)
    qblk = lambda b, j: (b * nblk + j, 0)
    full = lambda b, j: (b, 0)
    return pl.pallas_call(
        functools.partial(_dsa_kernel, k_sel=k_sel),
        grid=(batch, nblk),
        in_specs=[pl.BlockSpec((blk, 1024), qblk),
                  pl.BlockSpec((seq, KV_WIDTH), full),
                  pl.BlockSpec((nblk, KV_WIDTH, blk), lambda b, j: (b, 0, 0)),
                  pl.BlockSpec((1, IDX_HEADS, blk, IDX_DIM), lambda b, j: (b * nblk + j, 0, 0, 0)),
                  pl.BlockSpec((seq, IDX_DIM), full),
                  pl.BlockSpec((IDX_HEADS, blk), lambda b, j: (0, b * nblk + j))],
        out_specs=pl.BlockSpec((blk, 1024), qblk),
        out_shape=jax.ShapeDtypeStruct((n, 1024), F32),
        scratch_shapes=[pltpu.VMEM((nblk, blk, blk), F32),
                        pltpu.VMEM((nblk, blk, blk), F32),
                        pltpu.VMEM((N_KV_HEADS, HEAD_DIM, GROUP * blk), F32)],
        compiler_params=pltpu.CompilerParams(
            dimension_semantics=("arbitrary", "arbitrary"), vmem_limit_bytes=48 << 20),
        name="dsa",
    )(qn, kn, vt, qir, kir, wist)


def _mix_kernel(b_ref, c_ref, xt_ref, ch_ref, xth_ref, ga_ref, gb_ref, yb_ref, x_ref,
                cw_ref, wo_ref, g2_ref, wq_ref,
                h_ref, hn_ref, pq_ref, u_ref, *, blocks_per_seq):
    tm = b_ref.shape[0]
    i = pl.program_id(0)
    first = (i % blocks_per_seq) == 0
    halo = ch_ref[...] * xth_ref[...]
    u_ref[0:8, :] = jnp.where(first, 0.0, halo)
    u = c_ref[...] * xt_ref[...]
    u_ref[8:8 + tm, :] = u
    cw = cw_ref[...]
    conv = cw[0:1, :] * u_ref[6:6 + tm, :] + cw[1:2, :] * u_ref[7:7 + tm, :] + cw[2:3, :] * u
    y_a = b_ref[...] * conv
    merged = jax.nn.sigmoid(ga_ref[...]) * y_a + jax.nn.sigmoid(gb_ref[...]) * yb_ref[...]
    h = x_ref[...] + _dot(merged.astype(BF16), wo_ref[...])
    h_ref[...] = h
    hn = (h * lax.rsqrt(jnp.mean(h * h, axis=-1, keepdims=True) + EPS) * g2_ref[...]).astype(BF16)
    hn_ref[...] = hn
    pq_ref[...] = _dot(hn, wq_ref[...]).astype(BF16)


def _mix(proj, tail, yb, x2, conv_w, wo, g2, wq, seq, tm=256):
    n = x2.shape[0]
    col = lambda off: (lambda i: (i, off // 1024))
    halo = lambda off: (lambda i: (jnp.maximum(i * (tm // 8) - 1, 0), off // 1024))
    row = lambda i: (i, 0)
    const = lambda i: (0, 0)
    return pl.pallas_call(
        functools.partial(_mix_kernel, blocks_per_seq=seq // tm),
        grid=(n // tm,),
        in_specs=[pl.BlockSpec((tm, 1024), col(OFF_B)),
                  pl.BlockSpec((tm, 1024), col(OFF_C)),
                  pl.BlockSpec((tm, 1024), col(OFF_XT)),
                  pl.BlockSpec((8, 1024), halo(OFF_C)),
                  pl.BlockSpec((8, 1024), halo(OFF_XT)),
                  pl.BlockSpec((tm, 1024), col(OFF_GA)),
                  pl.BlockSpec((tm, 1024), col(OFF_GB)),
                  pl.BlockSpec((tm, 1024), row),
                  pl.BlockSpec((tm, 1024), row),
                  pl.BlockSpec((CONV_K, 1024), const),
                  pl.BlockSpec((1024, 1024), const),
                  pl.BlockSpec((1, 1024), const),
                  pl.BlockSpec((1024, 2048), const)],
        out_specs=[pl.BlockSpec((tm, 1024), row),
                   pl.BlockSpec((tm, 1024), row),
                   pl.BlockSpec((tm, 2048), row)],
        out_shape=[jax.ShapeDtypeStruct((n, 1024), F32),
                   jax.ShapeDtypeStruct((n, 1024), BF16),
                   jax.ShapeDtypeStruct((n, 2048), BF16)],
        scratch_shapes=[pltpu.VMEM((tm + 8, 1024), F32)],
        compiler_params=pltpu.CompilerParams(
            dimension_semantics=("arbitrary",), vmem_limit_bytes=48 << 20),
        name="mix",
    )(proj, proj, proj, proj, proj, tail, tail, yb, x2, conv_w, wo, g2, wq)


_CANDS = [(i, j) for i in range(PEER_TOPK) for j in range(PEER_TOPK) if (i + 1) * (j + 1) <= PEER_TOPK]


def _tree(op, xs):
    xs = list(xs)
    while len(xs) > 1:
        xs = [op(xs[i], xs[i + 1]) if i + 1 < len(xs) else xs[i] for i in range(0, len(xs), 2)]
    return xs[0]


def _oddeven_merge(lo, hi, r):
    step = r * 2
    if step < hi - lo:
        yield from _oddeven_merge(lo, hi, step)
        yield from _oddeven_merge(lo + r, hi, step)
        yield from [(i, i + r) for i in range(lo + r, hi - r, step)]
    else:
        yield (lo, lo + r)


def _oddeven_merge_sort(lo, hi):
    if hi - lo >= 1:
        mid = lo + (hi - lo) // 2
        yield from _oddeven_merge_sort(lo, mid)
        yield from _oddeven_merge_sort(mid + 1, hi)
        yield from _oddeven_merge(lo, hi, 1)


_SORT16 = list(_oddeven_merge_sort(0, PEER_TOPK - 1))
_BITONIC16 = [(i, i + d) for d in (8, 4, 2, 1) for i in range(PEER_TOPK) if not i & d]


def _hi(a, b):
    swap = b[0] > a[0]
    return (jnp.maximum(a[0], b[0]),) + tuple(jnp.where(swap, y, x) for x, y in zip(a[1:], b[1:]))


def _ce(a, b):
    swap = b[0] > a[0]
    lo = (jnp.minimum(a[0], b[0]),) + tuple(jnp.where(swap, x, y) for x, y in zip(a[1:], b[1:]))
    return _hi(a, b), lo


def _apply(items, pairs):
    for i, j in pairs:
        items[i], items[j] = _ce(items[i], items[j])
    return items


def _merge_top(a, b):
    m = len(b)
    c = [a[i] if i < PEER_TOPK - m else _hi(a[i], b[PEER_TOPK - 1 - i]) for i in range(PEER_TOPK)]
    return _apply(c, _BITONIC16)


def _peer_topk_kernel(pq_ref, kbig_ref, i1_ref, i2_ref, g_ref,
                      s_ref, val_ref, idx_ref, c_ref, top_ref, n1_ref, n2_ref):
    tm = pq_ref.shape[0]
    hp = PEER_HEADS
    halves = [slice(u * LANES, (u + 1) * LANES) for u in range(tm // LANES)]

    def any_true(flags):
        return jnp.max(_tree(jnp.maximum, flags)) > 0.0

    def tie_flag(top_vals, all_vals, ordered):
        cnt = _tree(jnp.add, [jnp.where(v >= top_vals[PEER_TOPK - 1], 1.0, 0.0) for v in all_vals])
        bad = jnp.where(cnt == float(PEER_TOPK), 0.0, 1.0)
        if ordered:
            bad = _tree(jnp.maximum, [bad] + [jnp.where(top_vals[r] > top_vals[r + 1], 0.0, 1.0)
                                              for r in range(PEER_TOPK - 1)])
        return bad

    for side in range(2):
        q = pq_ref[:, side * 1024:(side + 1) * 1024]
        s_ref[...] = _nt_dot(kbig_ref[side], q).reshape(N_KEYS, hp, tm)

        flags = []
        for ln in halves:
            rows = [s_ref[n, :, ln] for n in range(N_KEYS)]
            groups = [_apply([(rows[g * PEER_TOPK + r], g * PEER_TOPK + r) for r in range(PEER_TOPK)], _SORT16)
                      for g in range(N_KEYS // PEER_TOPK)]
            while len(groups) > 1:
                groups = [_merge_top(groups[u], groups[u + 1]) for u in range(0, len(groups), 2)]
            for r in range(PEER_TOPK):
                val_ref[side, r, :, ln] = groups[0][r][0]
                idx_ref[side, r, :, ln] = groups[0][r][1]
            flags.append(tie_flag([it[0] for it in groups[0]], rows, ordered=True))

        @pl.when(any_true(flags))
        def _():
            def extract(i, carry):
                rows = [s_ref[n] for n in range(N_KEYS)]
                m = _tree(jnp.maximum, rows)
                idx = _tree(jnp.minimum, [jnp.where(rows[n] == m, n, N_KEYS) for n in range(N_KEYS)])
                val_ref[side, i] = m
                idx_ref[side, i] = idx
                for n in range(N_KEYS):
                    s_ref[n] = jnp.where(idx == n, NEG_INF, rows[n])
                return carry

            lax.fori_loop(0, PEER_TOPK, extract, 0)

    flags = []
    for ln in halves:
        def cand(i, j):
            return (val_ref[0, i, :, ln] + val_ref[1, j, :, ln], idx_ref[0, i, :, ln], idx_ref[1, j, :, ln])

        g0 = [cand(0, j) for j in range(16)]
        g1 = _apply([cand(1, j) for j in range(8)] + [cand(i, 0) for i in range(15, 7, -1)], _BITONIC16)
        g2 = _apply([cand(i, j) for i, w in ((2, 5), (3, 4), (4, 3), (5, 2), (6, 2)) for j in range(w)], _SORT16)
        g3 = [cand(7, 0), cand(7, 1)]
        top = _merge_top(_merge_top(g0, g1), _merge_top(g2, g3))
        for k in range(PEER_TOPK):
            top_ref[k, :, ln] = top[k][0]
            n1_ref[k, :, ln] = top[k][1]
            n2_ref[k, :, ln] = top[k][2]
        flags.append(tie_flag([it[0] for it in top], [cand(i, j)[0] for i, j in _CANDS], ordered=False))

    @pl.when(any_true(flags))
    def _():
        for ci, (i, j) in enumerate(_CANDS):
            c_ref[ci] = val_ref[0, i] + val_ref[1, j]
        big = PEER_TOPK * PEER_TOPK

        def pick(k, carry):
            cs = [c_ref[ci] for ci in range(len(_CANDS))]
            m = _tree(jnp.maximum, cs)
            pos = _tree(jnp.minimum, [jnp.where(cs[ci] == m, i * PEER_TOPK + j, big)
                                      for ci, (i, j) in enumerate(_CANDS)])
            top_ref[k] = m
            isel = pos >> 4
            jsel = pos & (PEER_TOPK - 1)
            zero = jnp.zeros((hp, tm), I32)
            n1_ref[k] = _tree(jnp.add, [jnp.where(isel == r, idx_ref[0, r], zero) for r in range(PEER_TOPK)])
            n2_ref[k] = _tree(jnp.add, [jnp.where(jsel == r, idx_ref[1, r], zero) for r in range(PEER_TOPK)])
            for ci, (i, j) in enumerate(_CANDS):
                c_ref[ci] = jnp.where(pos == i * PEER_TOPK + j, NEG_INF, cs[ci])
            return carry

        lax.fori_loop(0, PEER_TOPK, pick, 0)

    top = top_ref[...]
    e = jnp.exp(top - top[0:1])
    gate = e / jnp.sum(e, axis=0, keepdims=True)
    i1_ref[...] = n1_ref[...].reshape(PEER_TOPK * hp, tm).T
    i2_ref[...] = n2_ref[...].reshape(PEER_TOPK * hp, tm).T
    g_ref[...] = gate.reshape(PEER_TOPK * hp, tm).T


def _peer_topk(pq, kbig, tm=256):
    n = pq.shape[0]
    r = PEER_TOPK * PEER_HEADS
    row = lambda i: (i, 0)
    ncand = len(_CANDS)
    return pl.pallas_call(
        _peer_topk_kernel,
        grid=(n // tm,),
        in_specs=[pl.BlockSpec((tm, 2048), row),
                  pl.BlockSpec((2, 1024, 1024), lambda i: (0, 0, 0))],
        out_specs=[pl.BlockSpec((tm, r), row)] * 3,
        out_shape=[jax.ShapeDtypeStruct((n, r), I32),
                   jax.ShapeDtypeStruct((n, r), I32),
                   jax.ShapeDtypeStruct((n, r), F32)],
        scratch_shapes=[pltpu.VMEM((N_KEYS, PEER_HEADS, tm), F32),
                        pltpu.VMEM((2, PEER_TOPK, PEER_HEADS, tm), F32),
                        pltpu.VMEM((2, PEER_TOPK, PEER_HEADS, tm), I32),
                        pltpu.VMEM((ncand, PEER_HEADS, tm), F32),
                        pltpu.VMEM((PEER_TOPK, PEER_HEADS, tm), F32),
                        pltpu.VMEM((PEER_TOPK, PEER_HEADS, tm), I32),
                        pltpu.VMEM((PEER_TOPK, PEER_HEADS, tm), I32)],
        compiler_params=pltpu.CompilerParams(dimension_semantics=("arbitrary",)),
        name="peer_topk",
    )(pq, kbig)


def _peer_w_kernel(i1_ref, i2_ref, g_ref, u_ref, v_ref, w_ref, ub_ref, vb_ref, scr_ref):
    tb = i1_ref.shape[0]
    ub_ref[...] = u_ref[...].astype(BF16)
    vb_ref[...] = v_ref[...].astype(BF16)
    sub = lax.broadcasted_iota(I32, (N_KEYS, LANES), 0)

    def token_group(tg, carry):
        for u in range(W_TOKENS_PER_ITER):
            t = tg * W_TOKENS_PER_ITER + u
            i1 = jnp.broadcast_to(i1_ref[pl.ds(t, 1), :], (N_KEYS, LANES))
            i2 = jnp.broadcast_to(i2_ref[pl.ds(t, 1), :], (N_KEYS, LANES))
            gv = jnp.broadcast_to(g_ref[pl.ds(t, 1), :], (N_KEYS, LANES))
            a = jnp.where(sub == i1, gv, 0.0).astype(BF16)
            bt = jnp.where(sub == i2, 1.0, 0.0).astype(BF16)
            scr_ref[pl.ds(t * W_SCRATCH_PITCH, N_KEYS), :] = _nt_dot(a, bt)
        return carry

    lax.fori_loop(0, tb // W_TOKENS_PER_ITER, token_group, 0)
    for n1 in range(N_KEYS):
        w_ref[n1] = scr_ref[pl.ds(n1, tb, stride=W_SCRATCH_PITCH), :].astype(BF16)


def _peer_w(i1, i2, g, peer_u, peer_v, tb=256):
    n = i1.shape[0]
    ne, d = peer_u.shape
    steps = n // tb
    assert ne % steps == 0, (ne, steps)
    er = ne // steps
    row = lambda i: (i, 0)
    return pl.pallas_call(
        _peer_w_kernel,
        grid=(steps,),
        in_specs=[pl.BlockSpec((tb, 128), row)] * 3 + [pl.BlockSpec((er, d), row)] * 2,
        out_specs=[pl.BlockSpec((N_KEYS, tb, LANES), lambda i: (0, i, 0)),
                   pl.BlockSpec((er, d), row), pl.BlockSpec((er, d), row)],
        out_shape=[jax.ShapeDtypeStruct((N_KEYS, n, LANES), BF16),
                   jax.ShapeDtypeStruct((ne, d), BF16), jax.ShapeDtypeStruct((ne, d), BF16)],
        scratch_shapes=[pltpu.VMEM((tb * W_SCRATCH_PITCH, LANES), F32)],
        compiler_params=pltpu.CompilerParams(
            dimension_semantics=("arbitrary",), vmem_limit_bytes=48 << 20),
        name="peer_w",
    )(i1, i2, g, peer_u, peer_v)


def _peer_ffn_kernel(hn_ref, u_ref, v_ref, w_ref, h_ref, o_ref):
    k = pl.program_id(1)
    nsub = w_ref.shape[0]

    @pl.when(k == 0)
    def _():
        o_ref[...] = h_ref[...]

    pre = _nt_dot(hn_ref[...], u_ref[...])
    act = 0.5 * pre * (1.0 + lax.erf(pre * (1.0 / math.sqrt(2.0))))
    w = jnp.concatenate([w_ref[s] for s in range(nsub)], axis=1).astype(F32)
    o_ref[...] += _dot((w * act).astype(BF16), v_ref[...])


def _peer_ffn(hn, ub, vb, w, h, tm=1024, te=1024):
    n = hn.shape[0]
    ne = ub.shape[0]
    return pl.pallas_call(
        _peer_ffn_kernel,
        grid=(n // tm, ne // te),
        in_specs=[pl.BlockSpec((tm, 1024), lambda i, k: (i, 0)),
                  pl.BlockSpec((te, 1024), lambda i, k: (k, 0)),
                  pl.BlockSpec((te, 1024), lambda i, k: (k, 0)),
                  pl.BlockSpec((te // N_KEYS, tm, LANES), lambda i, k: (k, i, 0)),
                  pl.BlockSpec((tm, 1024), lambda i, k: (i, 0))],
        out_specs=pl.BlockSpec((tm, 1024), lambda i, k: (i, 0)),
        out_shape=jax.ShapeDtypeStruct((n, 1024), F32),
        compiler_params=pltpu.CompilerParams(
            dimension_semantics=("arbitrary", "arbitrary"), vmem_limit_bytes=48 << 20),
        name="peer_ffn",
    )(hn, ub, vb, w, h)


def _rope_tables(seq, dim, width, reps):
    r = dim // ROPE_FRACTION
    half = r // 2
    inv_freq = ROPE_THETA ** (-jnp.arange(half, dtype=F32) / half)
    ang = jnp.arange(seq).astype(F32)[:, None] * inv_freq[None, :]
    cos, sin = jnp.cos(ang), jnp.sin(ang)
    rest1 = jnp.ones((seq, width - r), F32)
    rest0 = jnp.zeros((seq, width - r), F32)
    zh = jnp.zeros((seq, half), F32)
    c = jnp.concatenate([cos, cos, rest1], axis=1)
    sa = jnp.concatenate([-sin, zh, rest0], axis=1)
    sb = jnp.concatenate([zh, sin, rest0], axis=1)
    pad1 = jnp.ones((seq, LANES - width * reps), F32)
    pad0 = jnp.zeros((seq, LANES - width * reps), F32)
    c = jnp.concatenate([c] * reps + [pad1], axis=1)
    sa = jnp.concatenate([sa] * reps + [pad0], axis=1)
    sb = jnp.concatenate([sb] * reps + [pad0], axis=1)
    return jnp.stack([c, sa, sb])


def kernel(x, norm1_g, w_in, conv_w, q_norm_g, k_norm_g, w_o, norm2_g,
           peer_wq, peer_k1, peer_k2, peer_u, peer_v):
    batch, seq, d = x.shape
    n = batch * seq
    assert d == D_MODEL and seq % DSA_BLOCK == 0 and n % 1024 == 0, (batch, seq, d)
    x2 = x.reshape(n, d)

    assert sum(COL_SIZES) == w_in.shape[1] and sum(COL_SIZES[:7]) == MAIN_COLS
    w_in_b = w_in.astype(BF16)
    gates_at = MAIN_COLS + IDX_DIM + IDX_HEADS
    pad = jnp.zeros((d, LANES - IDX_DIM - IDX_HEADS), BF16)
    w_tail = jnp.concatenate([w_in_b[:, gates_at:], w_in_b[:, MAIN_COLS:gates_at], pad], axis=1)

    wq_p = peer_wq.astype(BF16).reshape(d, PEER_HEADS, 2, PEER_HALF).transpose(0, 2, 1, 3).reshape(d, 2048)
    eye = jnp.eye(PEER_HEADS, dtype=BF16)

    def big(kk):
        return jnp.einsum('hnd,hg->nhgd', kk.astype(BF16), eye).reshape(
            N_KEYS * PEER_HEADS, PEER_HEADS * PEER_HALF)

    kbig = jnp.stack([big(peer_k1), big(peer_k2)])

    t128 = _rope_tables(seq, HEAD_DIM, HEAD_DIM, 1)
    t64 = _rope_tables(seq, IDX_DIM, IDX_DIM, 2)
    tki = _rope_tables(seq, IDX_DIM, IDX_DIM, 1)

    g1 = norm1_g.reshape(1, d)
    proj = _in_proj(x2, g1, w_in_b, MAIN_COLS, MAIN_COLS // 2, "in_proj")
    tail = _in_proj(x2, g1, w_tail, TAIL_COLS, TAIL_COLS, "in_proj_tail")
    qn, kn, vt, qir, kir, wist = _prep(proj, tail, t128, t64, tki, q_norm_g.reshape(1, HEAD_DIM),
                                       k_norm_g.reshape(1, HEAD_DIM), seq)
    yb = _dsa(qn, kn, vt, qir, kir, wist, batch, seq)
    h, hn, pq = _mix(proj, tail, yb, x2, conv_w, w_o.astype(BF16), norm2_g.reshape(1, d), wq_p, seq)
    i1, i2, g = _peer_topk(pq, kbig)
    w, ub, vb = _peer_w(i1, i2, g, peer_u, peer_v)
    out = _peer_ffn(hn, ub, vb, w, h)
    return out.reshape(batch, seq, d)
```

```python
import functools
import math

import jax
import jax.numpy as jnp
from jax import lax
from jax.experimental import pallas as pl
from jax.experimental.pallas import tpu as pltpu

F32 = jnp.float32
BF16 = jnp.bfloat16
I32 = jnp.int32

D_MODEL = 1024
EPS = 1e-6
CONV_K = 3
N_HEADS = 8
HEAD_DIM = 128
N_KV_HEADS = 2
GROUP = N_HEADS // N_KV_HEADS
KV_WIDTH = N_KV_HEADS * HEAD_DIM
IDX_HEADS = 8
IDX_DIM = 64
INDEX_TOPK = 256
ATTN_SCALE = HEAD_DIM ** -0.5
IDX_SCALE = IDX_DIM ** -0.5
IDX_W_SCALE = IDX_HEADS ** -0.5
ROPE_THETA = 500000.0
ROPE_FRACTION = 4
PEER_HEADS = 8
PEER_HALF = 128
N_KEYS = 128
PEER_TOPK = 16

LANES = 128
INT_MIN = -(2 ** 31)
KEY_NEG_INF = INT_MIN + 0x7FFFFF
NEG_INF = float("-inf")

COL_SIZES = (1024, 1024, 1024, 1024, 256, 256, 512, 64, 8, 1024, 1024)
OFF_B, OFF_C, OFF_XT, OFF_Q, OFF_K, OFF_V, OFF_QI = 0, 1024, 2048, 3072, 4096, 4352, 4608
MAIN_COLS = 5120
OFF_GA, OFF_GB, OFF_KIWI = 0, 1024, 2048
TAIL_COLS = 2176
W_SCRATCH_PITCH = 132
W_TOKENS_PER_ITER = 64
DSA_BLOCK = 256
M_FLOOR = -1e30

NT_DIMS = (((1,), (1,)), ((), ()))


def _nt_dot(a, b):
    return lax.dot_general(a, b, NT_DIMS, preferred_element_type=F32)


def _dot(a, b):
    return jnp.dot(a, b, preferred_element_type=F32)


def _in_proj_kernel(x_ref, g_ref, w_ref, o_ref, wb_ref):
    @pl.when(pl.program_id(1) == 0)
    def _():
        wb_ref[...] = w_ref[...].astype(BF16)

    x = x_ref[...]
    xn = x * lax.rsqrt(jnp.mean(x * x, axis=-1, keepdims=True) + EPS) * g_ref[...]
    o_ref[...] = _dot(xn.astype(BF16), wb_ref[...])


def _in_proj(x2, g, w, cols, tn, name, tm=512):
    n = x2.shape[0]
    return pl.pallas_call(
        _in_proj_kernel,
        grid=(cols // tn, n // tm),
        in_specs=[pl.BlockSpec((tm, D_MODEL), lambda j, i: (i, 0)),
                  pl.BlockSpec((1, D_MODEL), lambda j, i: (0, 0)),
                  pl.BlockSpec((D_MODEL, tn), lambda j, i: (0, j))],
        out_specs=pl.BlockSpec((tm, tn), lambda j, i: (i, j)),
        out_shape=jax.ShapeDtypeStruct((n, cols), F32),
        scratch_shapes=[pltpu.VMEM((D_MODEL, tn), BF16)],
        compiler_params=pltpu.CompilerParams(
            dimension_semantics=("arbitrary", "arbitrary"), vmem_limit_bytes=48 << 20),
        name=name,
    )(x2, g, w)


def _rope(x, c, sa, sb, shift):
    left = pltpu.roll(x, LANES - shift, 1)
    right = pltpu.roll(x, shift, 1)
    return x * c + left * sa + right * sb


def _head_norm(x, g):
    return x * lax.rsqrt(jnp.mean(x * x, axis=-1, keepdims=True) + EPS) * g


def _prep_kernel(q_ref, k_ref, v_ref, qi_ref, kiwi_ref, t128_ref, t64_ref, tki_ref,
                 qg_ref, kg_ref,
                 qn_ref, kn_ref, vt_ref, qir_ref, kir_ref, wist_ref):
    c128, sa128, sb128 = t128_ref[0], t128_ref[1], t128_ref[2]
    c64, sa64, sb64 = t64_ref[0], t64_ref[1], t64_ref[2]
    cki, saki, sbki = tki_ref[0], tki_ref[1], tki_ref[2]
    qg = qg_ref[...]
    kg = kg_ref[...]
    for h in range(N_HEADS):
        sl = slice(h * HEAD_DIM, (h + 1) * HEAD_DIM)
        qh = _head_norm(q_ref[:, sl], qg)
        qn_ref[:, sl] = _rope(qh, c128, sa128, sb128, 16).astype(BF16)
    for h in range(N_KV_HEADS):
        sl = slice(h * HEAD_DIM, (h + 1) * HEAD_DIM)
        kh = _head_norm(k_ref[:, sl], kg)
        kn_ref[:, sl] = _rope(kh, c128, sa128, sb128, 16).astype(BF16)
    vt_ref[0] = v_ref[...].T.astype(BF16)
    for p in range(IDX_HEADS // 2):
        sl = slice(p * LANES, (p + 1) * LANES)
        pair = _rope(qi_ref[:, sl], c64, sa64, sb64, 8).astype(BF16)
        qir_ref[0, 2 * p] = pair[:, :IDX_DIM]
        qir_ref[0, 2 * p + 1] = pair[:, IDX_DIM:]
    kiwi = kiwi_ref[...]
    kir_ref[...] = _rope(kiwi, cki, saki, sbki, 8)[:, :IDX_DIM].astype(BF16)
    wist_ref[...] = (kiwi * IDX_W_SCALE * IDX_SCALE).T[IDX_DIM:IDX_DIM + IDX_HEADS, :]


def _prep(proj, tail, t128, t64, tki, qg, kg, seq):
    tm = DSA_BLOCK
    n = proj.shape[0]
    sblk = seq // tm
    tab = lambda i: (0, i % sblk, 0)
    return pl.pallas_call(
        _prep_kernel,
        grid=(n // tm,),
        in_specs=[pl.BlockSpec((tm, 1024), lambda i: (i, OFF_Q // 1024)),
                  pl.BlockSpec((tm, 256), lambda i: (i, OFF_K // 256)),
                  pl.BlockSpec((tm, 256), lambda i: (i, OFF_V // 256)),
                  pl.BlockSpec((tm, 512), lambda i: (i, OFF_QI // 512)),
                  pl.BlockSpec((tm, 128), lambda i: (i, OFF_KIWI // 128)),
                  pl.BlockSpec((3, tm, 128), tab),
                  pl.BlockSpec((3, tm, 128), tab),
                  pl.BlockSpec((3, tm, 128), tab),
                  pl.BlockSpec((1, 128), lambda i: (0, 0)),
                  pl.BlockSpec((1, 128), lambda i: (0, 0))],
        out_specs=[pl.BlockSpec((tm, 1024), lambda i: (i, 0)),
                   pl.BlockSpec((tm, KV_WIDTH), lambda i: (i, 0)),
                   pl.BlockSpec((1, KV_WIDTH, tm), lambda i: (i, 0, 0)),
                   pl.BlockSpec((1, IDX_HEADS, tm, IDX_DIM), lambda i: (i, 0, 0, 0)),
                   pl.BlockSpec((tm, IDX_DIM), lambda i: (i, 0)),
                   pl.BlockSpec((IDX_HEADS, tm), lambda i: (0, i))],
        out_shape=[jax.ShapeDtypeStruct((n, 1024), BF16),
                   jax.ShapeDtypeStruct((n, KV_WIDTH), BF16),
                   jax.ShapeDtypeStruct((n // tm, KV_WIDTH, tm), BF16),
                   jax.ShapeDtypeStruct((n // tm, IDX_HEADS, tm, IDX_DIM), BF16),
                   jax.ShapeDtypeStruct((n, IDX_DIM), BF16),
                   jax.ShapeDtypeStruct((IDX_HEADS, n), F32)],
        compiler_params=pltpu.CompilerParams(dimension_semantics=("arbitrary",)),
        name="prep",
    )(proj, proj, proj, proj, tail, t128, t64, tki, qg, kg)


def _dsa_kernel(qn_ref, kn_ref, vt_ref, qir_ref, kir_ref, wist_ref, o_ref,
                keys_ref, bias_ref, acc_ref, *, k_sel):
    j = pl.program_id(1)
    nch = j + 1
    blk = DSA_BLOCK
    kf = float(k_sel)

    def rowsum(x):
        return jnp.sum(x, axis=0, keepdims=True)

    wist = wist_ref[...]
    qi = qir_ref[0].reshape(IDX_HEADS * blk, IDX_DIM)
    diff = lax.broadcasted_iota(I32, (blk, blk), 0) - lax.broadcasted_iota(I32, (blk, blk), 1)

    def score_chunk(c, carry):
        kc = kir_ref[pl.ds(pl.multiple_of(c * blk, blk), blk), :]
        d = jnp.maximum(_nt_dot(kc, qi), 0.0)
        acc = wist[0:1, :] * d[:, 0:blk]
        for h in range(1, IDX_HEADS):
            acc = acc + wist[h:h + 1, :] * d[:, h * blk:(h + 1) * blk]
        keys_ref[c] = jnp.where(diff <= (j - c) * blk, acc, NEG_INF)
        return carry

    lax.fori_loop(0, nch, score_chunk, 0)

    def count(pred):
        def body(c, cnt):
            m = jnp.where(pred(keys_ref[c]), 1.0, 0.0)
            return cnt + jnp.sum(m.reshape(blk // 8, 8, blk), axis=0)
        return rowsum(lax.fori_loop(0, nch, body, jnp.zeros((8, blk), F32)))

    def as_float(t):
        t = jnp.maximum(t, KEY_NEG_INF)
        return pltpu.bitcast(t ^ ((t >> 31) & 0x7FFFFFFF), F32)

    c0 = count(lambda k: k >= 0.0)
    t0 = jnp.where(c0 >= kf, 0, INT_MIN).astype(I32)
    n0 = jnp.where(c0 >= kf, c0, (nch * blk).astype(F32))

    def bit_body(i, carry):
        t, n_ge = carry
        cand = t | jnp.left_shift(jnp.int32(1), 30 - i)
        cf = as_float(cand)
        cnt = count(lambda k: k >= cf)
        return jnp.where(cnt >= kf, cand, t), jnp.where(cnt >= kf, cnt, n_ge)

    thr_key, n_ge = lax.fori_loop(0, 31, bit_body, (t0, n0))
    thr = as_float(thr_key)
    ties = jnp.max(n_ge) > kf

    @pl.when(jnp.logical_not(ties))
    def _():
        def mask_chunk(c, carry):
            bias = jnp.where(keys_ref[c] >= thr, 0.0, NEG_INF)
            bias_ref[c] = jnp.where(diff <= (j - c) * blk, bias, NEG_INF)
            return carry

        lax.fori_loop(0, nch, mask_chunk, 0)

    @pl.when(ties)
    def _():
        need = kf - count(lambda k: k > thr)
        tri = jnp.where(diff >= 0, 1.0, 0.0).astype(BF16)

        def mask_chunk(c, seen):
            key = keys_ref[c]
            eq = jnp.where(key == thr, 1.0, 0.0)
            rank = seen + _dot(tri, eq.astype(BF16))
            tie_bias = jnp.where(rank <= need, 0.0, NEG_INF)
            eq_bias = jnp.where(eq > 0.0, tie_bias, NEG_INF)
            bias = jnp.where(key > thr, 0.0, eq_bias)
            bias_ref[c] = jnp.where(diff <= (j - c) * blk, bias, NEG_INF)
            return seen + rowsum(eq)

        lax.fori_loop(0, nch, mask_chunk, jnp.zeros((1, blk), F32))

    qs = [jnp.concatenate(
        [qn_ref[:, (g * GROUP + n) * HEAD_DIM:(g * GROUP + n + 1) * HEAD_DIM] for n in range(GROUP)],
        axis=0) for g in range(N_KV_HEADS)]
    acc_ref[...] = jnp.zeros(acc_ref.shape, F32)

    def attend(c, carry):
        rows = pl.ds(pl.multiple_of(c * blk, blk), blk)
        bias = jnp.concatenate([bias_ref[c]] * GROUP, axis=1)
        out = []
        for g in range(N_KV_HEADS):
            m, l = carry[g]
            gsl = slice(g * HEAD_DIM, (g + 1) * HEAD_DIM)
            s = _nt_dot(kn_ref[rows, gsl], qs[g]) * ATTN_SCALE + bias
            m_new = jnp.maximum(m, jnp.max(s, axis=0, keepdims=True))
            alpha = jnp.exp(m - m_new)
            p = jnp.exp(s - m_new)
            acc_ref[g] = acc_ref[g] * alpha + _dot(vt_ref[c, gsl, :], p.astype(BF16))
            out.append((m_new, l * alpha + rowsum(p)))
        return tuple(out)

    init = tuple((jnp.full((1, GROUP * blk), M_FLOOR, F32), jnp.zeros((1, GROUP * blk), F32))
                 for _ in range(N_KV_HEADS))
    stats = lax.fori_loop(0, nch, attend, init)
    for g in range(N_KV_HEADS):
        ot = acc_ref[g] / stats[g][1]
        for n in range(GROUP):
            hsl = slice((g * GROUP + n) * HEAD_DIM, (g * GROUP + n + 1) * HEAD_DIM)
            o_ref[:, hsl] = ot[:, n * blk:(n + 1) * blk].T


def _dsa(qn, kn, vt, qir, kir, wist, batch, seq):
    blk = DSA_BLOCK
    n = qn.shape[0]
    nblk = seq // blk
    k_sel = min(INDEX_TOPK, seq // 4)
    qblk = lambda b, j: (b * nblk + j, 0)
    full = lambda b, j: (b, 0)
    return pl.pallas_call(
        functools.partial(_dsa_kernel, k_sel=k_sel),
        grid=(batch, nblk),
        in_specs=[pl.BlockSpec((blk, 1024), qblk),
                  pl.BlockSpec((seq, KV_WIDTH), full),
                  pl.BlockSpec((nblk, KV_WIDTH, blk), lambda b, j: (b, 0, 0)),
                  pl.BlockSpec((1, IDX_HEADS, blk, IDX_DIM), lambda b, j: (b * nblk + j, 0, 0, 0)),
                  pl.BlockSpec((seq, IDX_DIM), full),
                  pl.BlockSpec((IDX_HEADS, blk), lambda b, j: (0, b * nblk + j))],
        out_specs=pl.BlockSpec((blk, 1024), qblk),
        out_shape=jax.ShapeDtypeStruct((n, 1024), F32),
        scratch_shapes=[pltpu.VMEM((nblk, blk, blk), F32),
                        pltpu.VMEM((nblk, blk, blk), F32),
                        pltpu.VMEM((N_KV_HEADS, HEAD_DIM, GROUP * blk), F32)],
        compiler_params=pltpu.CompilerParams(
            dimension_semantics=("arbitrary", "arbitrary"), vmem_limit_bytes=48 << 20),
        name="dsa",
    )(qn, kn, vt, qir, kir, wist)


def _mix_kernel(b_ref, c_ref, xt_ref, ch_ref, xth_ref, ga_ref, gb_ref, yb_ref, x_ref,
                cw_ref, wo_ref, g2_ref, wq_ref,
                h_ref, hn_ref, pq_ref, u_ref, *, blocks_per_seq):
    tm = b_ref.shape[0]
    i = pl.program_id(0)
    first = (i % blocks_per_seq) == 0
    halo = ch_ref[...] * xth_ref[...]
    u_ref[0:8, :] = jnp.where(first, 0.0, halo)
    u = c_ref[...] * xt_ref[...]
    u_ref[8:8 + tm, :] = u
    cw = cw_ref[...]
    conv = cw[0:1, :] * u_ref[6:6 + tm, :] + cw[1:2, :] * u_ref[7:7 + tm, :] + cw[2:3, :] * u
    y_a = b_ref[...] * conv
    merged = jax.nn.sigmoid(ga_ref[...]) * y_a + jax.nn.sigmoid(gb_ref[...]) * yb_ref[...]
    h = x_ref[...] + _dot(merged.astype(BF16), wo_ref[...])
    h_ref[...] = h
    hn = (h * lax.rsqrt(jnp.mean(h * h, axis=-1, keepdims=True) + EPS) * g2_ref[...]).astype(BF16)
    hn_ref[...] = hn
    pq_ref[...] = _dot(hn, wq_ref[...]).astype(BF16)


def _mix(proj, tail, yb, x2, conv_w, wo, g2, wq, seq, tm=256):
    n = x2.shape[0]
    col = lambda off: (lambda i: (i, off // 1024))
    halo = lambda off: (lambda i: (jnp.maximum(i * (tm // 8) - 1, 0), off // 1024))
    row = lambda i: (i, 0)
    const = lambda i: (0, 0)
    return pl.pallas_call(
        functools.partial(_mix_kernel, blocks_per_seq=seq // tm),
        grid=(n // tm,),
        in_specs=[pl.BlockSpec((tm, 1024), col(OFF_B)),
                  pl.BlockSpec((tm, 1024), col(OFF_C)),
                  pl.BlockSpec((tm, 1024), col(OFF_XT)),
                  pl.BlockSpec((8, 1024), halo(OFF_C)),
                  pl.BlockSpec((8, 1024), halo(OFF_XT)),
                  pl.BlockSpec((tm, 1024), col(OFF_GA)),
                  pl.BlockSpec((tm, 1024), col(OFF_GB)),
                  pl.BlockSpec((tm, 1024), row),
                  pl.BlockSpec((tm, 1024), row),
                  pl.BlockSpec((CONV_K, 1024), const),
                  pl.BlockSpec((1024, 1024), const),
                  pl.BlockSpec((1, 1024), const),
                  pl.BlockSpec((1024, 2048), const)],
        out_specs=[pl.BlockSpec((tm, 1024), row),
                   pl.BlockSpec((tm, 1024), row),
                   pl.BlockSpec((tm, 2048), row)],
        out_shape=[jax.ShapeDtypeStruct((n, 1024), F32),
                   jax.ShapeDtypeStruct((n, 1024), BF16),
                   jax.ShapeDtypeStruct((n, 2048), BF16)],
        scratch_shapes=[pltpu.VMEM((tm + 8, 1024), F32)],
        compiler_params=pltpu.CompilerParams(
            dimension_semantics=("arbitrary",), vmem_limit_bytes=48 << 20),
        name="mix",
    )(proj, proj, proj, proj, proj, tail, tail, yb, x2, conv_w, wo, g2, wq)


_CANDS = [(i, j) for i in range(PEER_TOPK) for j in range(PEER_TOPK) if (i + 1) * (j + 1) <= PEER_TOPK]


def _tree(op, xs):
    xs = list(xs)
    while len(xs) > 1:
        xs = [op(xs[i], xs[i + 1]) if i + 1 < len(xs) else xs[i] for i in range(0, len(xs), 2)]
    return xs[0]


def _oddeven_merge(lo, hi, r):
    step = r * 2
    if step < hi - lo:
        yield from _oddeven_merge(lo, hi, step)
        yield from _oddeven_merge(lo + r, hi, step)
        yield from [(i, i + r) for i in range(lo + r, hi - r, step)]
    else:
        yield (lo, lo + r)


def _oddeven_merge_sort(lo, hi):
    if hi - lo >= 1:
        mid = lo + (hi - lo) // 2
        yield from _oddeven_merge_sort(lo, mid)
        yield from _oddeven_merge_sort(mid + 1, hi)
        yield from _oddeven_merge(lo, hi, 1)


_SORT16 = list(_oddeven_merge_sort(0, PEER_TOPK - 1))
_BITONIC16 = [(i, i + d) for d in (8, 4, 2, 1) for i in range(PEER_TOPK) if not i & d]


def _hi(a, b):
    swap = b[0] > a[0]
    return (jnp.maximum(a[0], b[0]),) + tuple(jnp.where(swap, y, x) for x, y in zip(a[1:], b[1:]))


def _ce(a, b):
    swap = b[0] > a[0]
    lo = (jnp.minimum(a[0], b[0]),) + tuple(jnp.where(swap, x, y) for x, y in zip(a[1:], b[1:]))
    return _hi(a, b), lo


def _apply(items, pairs):
    for i, j in pairs:
        items[i], items[j] = _ce(items[i], items[j])
    return items


def _merge_top(a, b):
    m = len(b)
    c = [a[i] if i < PEER_TOPK - m else _hi(a[i], b[PEER_TOPK - 1 - i]) for i in range(PEER_TOPK)]
    return _apply(c, _BITONIC16)


def _peer_topk_kernel(pq_ref, kbig_ref, i1_ref, i2_ref, g_ref,
                      s_ref, val_ref, idx_ref, c_ref, top_ref, n1_ref, n2_ref):
    tm = pq_ref.shape[0]
    hp = PEER_HEADS
    halves = [slice(u * LANES, (u + 1) * LANES) for u in range(tm // LANES)]

    def any_true(flags):
        return jnp.max(_tree(jnp.maximum, flags)) > 0.0

    def tie_flag(top_vals, all_vals, ordered):
        cnt = _tree(jnp.add, [jnp.where(v >= top_vals[PEER_TOPK - 1], 1.0, 0.0) for v in all_vals])
        bad = jnp.where(cnt == float(PEER_TOPK), 0.0, 1.0)
        if ordered:
            bad = _tree(jnp.maximum, [bad] + [jnp.where(top_vals[r] > top_vals[r + 1], 0.0, 1.0)
                                              for r in range(PEER_TOPK - 1)])
        return bad

    for side in range(2):
        q = pq_ref[:, side * 1024:(side + 1) * 1024]
        s_ref[...] = _nt_dot(kbig_ref[side], q).reshape(N_KEYS, hp, tm)

        flags = []
        for ln in halves:
            rows = [s_ref[n, :, ln] for n in range(N_KEYS)]
            groups = [_apply([(rows[g * PEER_TOPK + r], g * PEER_TOPK + r) for r in range(PEER_TOPK)], _SORT16)
                      for g in range(N_KEYS // PEER_TOPK)]
            while len(groups) > 1:
                groups = [_merge_top(groups[u], groups[u + 1]) for u in range(0, len(groups), 2)]
            for r in range(PEER_TOPK):
                val_ref[side, r, :, ln] = groups[0][r][0]
                idx_ref[side, r, :, ln] = groups[0][r][1]
            flags.append(tie_flag([it[0] for it in groups[0]], rows, ordered=True))

        @pl.when(any_true(flags))
        def _():
            def extract(i, carry):
                rows = [s_ref[n] for n in range(N_KEYS)]
                m = _tree(jnp.maximum, rows)
                idx = _tree(jnp.minimum, [jnp.where(rows[n] == m, n, N_KEYS) for n in range(N_KEYS)])
                val_ref[side, i] = m
                idx_ref[side, i] = idx
                for n in range(N_KEYS):
                    s_ref[n] = jnp.where(idx == n, NEG_INF, rows[n])
                return carry

            lax.fori_loop(0, PEER_TOPK, extract, 0)

    flags = []
    for ln in halves:
        def cand(i, j):
            return (val_ref[0, i, :, ln] + val_ref[1, j, :, ln], idx_ref[0, i, :, ln], idx_ref[1, j, :, ln])

        g0 = [cand(0, j) for j in range(16)]
        g1 = _apply([cand(1, j) for j in range(8)] + [cand(i, 0) for i in range(15, 7, -1)], _BITONIC16)
        g2 = _apply([cand(i, j) for i, w in ((2, 5), (3, 4), (4, 3), (5, 2), (6, 2)) for j in range(w)], _SORT16)
        g3 = [cand(7, 0), cand(7, 1)]
        top = _merge_top(_merge_top(g0, g1), _merge_top(g2, g3))
        for k in range(PEER_TOPK):
            top_ref[k, :, ln] = top[k][0]
            n1_ref[k, :, ln] = top[k][1]
            n2_ref[k, :, ln] = top[k][2]
        flags.append(tie_flag([it[0] for it in top], [cand(i, j)[0] for i, j in _CANDS], ordered=False))

    @pl.when(any_true(flags))
    def _():
        for ci, (i, j) in enumerate(_CANDS):
            c_ref[ci] = val_ref[0, i] + val_ref[1, j]
        big = PEER_TOPK * PEER_TOPK

        def pick(k, carry):
            cs = [c_ref[ci] for ci in range(len(_CANDS))]
            m = _tree(jnp.maximum, cs)
            pos = _tree(jnp.minimum, [jnp.where(cs[ci] == m, i * PEER_TOPK + j, big)
                                      for ci, (i, j) in enumerate(_CANDS)])
            top_ref[k] = m
            isel = pos >> 4
            jsel = pos & (PEER_TOPK - 1)
            zero = jnp.zeros((hp, tm), I32)
            n1_ref[k] = _tree(jnp.add, [jnp.where(isel == r, idx_ref[0, r], zero) for r in range(PEER_TOPK)])
            n2_ref[k] = _tree(jnp.add, [jnp.where(jsel == r, idx_ref[1, r], zero) for r in range(PEER_TOPK)])
            for ci, (i, j) in enumerate(_CANDS):
                c_ref[ci] = jnp.where(pos == i * PEER_TOPK + j, NEG_INF, cs[ci])
            return carry

        lax.fori_loop(0, PEER_TOPK, pick, 0)

    top = top_ref[...]
    e = jnp.exp(top - top[0:1])
    gate = e / jnp.sum(e, axis=0, keepdims=True)
    i1_ref[...] = n1_ref[...].reshape(PEER_TOPK * hp, tm).T
    i2_ref[...] = n2_ref[...].reshape(PEER_TOPK * hp, tm).T
    g_ref[...] = gate.reshape(PEER_TOPK * hp, tm).T


def _peer_topk(pq, kbig, tm=256):
    n = pq.shape[0]
    r = PEER_TOPK * PEER_HEADS
    row = lambda i: (i, 0)
    ncand = len(_CANDS)
    return pl.pallas_call(
        _peer_topk_kernel,
        grid=(n // tm,),
        in_specs=[pl.BlockSpec((tm, 2048), row),
                  pl.BlockSpec((2, 1024, 1024), lambda i: (0, 0, 0))],
        out_specs=[pl.BlockSpec((tm, r), row)] * 3,
        out_shape=[jax.ShapeDtypeStruct((n, r), I32),
                   jax.ShapeDtypeStruct((n, r), I32),
                   jax.ShapeDtypeStruct((n, r), F32)],
        scratch_shapes=[pltpu.VMEM((N_KEYS, PEER_HEADS, tm), F32),
                        pltpu.VMEM((2, PEER_TOPK, PEER_HEADS, tm), F32),
                        pltpu.VMEM((2, PEER_TOPK, PEER_HEADS, tm), I32),
                        pltpu.VMEM((ncand, PEER_HEADS, tm), F32),
                        pltpu.VMEM((PEER_TOPK, PEER_HEADS, tm), F32),
                        pltpu.VMEM((PEER_TOPK, PEER_HEADS, tm), I32),
                        pltpu.VMEM((PEER_TOPK, PEER_HEADS, tm), I32)],
        compiler_params=pltpu.CompilerParams(dimension_semantics=("arbitrary",)),
        name="peer_topk",
    )(pq, kbig)


def _peer_w_kernel(i1_ref, i2_ref, g_ref, u_ref, v_ref, w_ref, ub_ref, vb_ref, scr_ref):
    tb = i1_ref.shape[0]
    ub_ref[...] = u_ref[...].astype(BF16)
    vb_ref[...] = v_ref[...].astype(BF16)
    sub = lax.broadcasted_iota(I32, (N_KEYS, LANES), 0)

    def token_group(tg, carry):
        for u in range(W_TOKENS_PER_ITER):
            t = tg * W_TOKENS_PER_ITER + u
            i1 = jnp.broadcast_to(i1_ref[pl.ds(t, 1), :], (N_KEYS, LANES))
            i2 = jnp.broadcast_to(i2_ref[pl.ds(t, 1), :], (N_KEYS, LANES))
            gv = jnp.broadcast_to(g_ref[pl.ds(t, 1), :], (N_KEYS, LANES))
            a = jnp.where(sub == i1, gv, 0.0).astype(BF16)
            bt = jnp.where(sub == i2, 1.0, 0.0).astype(BF16)
            scr_ref[pl.ds(t * W_SCRATCH_PITCH, N_KEYS), :] = _nt_dot(a, bt)
        return carry

    lax.fori_loop(0, tb // W_TOKENS_PER_ITER, token_group, 0)
    for n1 in range(N_KEYS):
        w_ref[n1] = scr_ref[pl.ds(n1, tb, stride=W_SCRATCH_PITCH), :].astype(BF16)


def _peer_w(i1, i2, g, peer_u, peer_v, tb=256):
    n = i1.shape[0]
    ne, d = peer_u.shape
    steps = n // tb
    assert ne % steps == 0, (ne, steps)
    er = ne // steps
    row = lambda i: (i, 0)
    return pl.pallas_call(
        _peer_w_kernel,
        grid=(steps,),
        in_specs=[pl.BlockSpec((tb, 128), row)] * 3 + [pl.BlockSpec((er, d), row)] * 2,
        out_specs=[pl.BlockSpec((N_KEYS, tb, LANES), lambda i: (0, i, 0)),
                   pl.BlockSpec((er, d), row), pl.BlockSpec((er, d), row)],
        out_shape=[jax.ShapeDtypeStruct((N_KEYS, n, LANES), BF16),
                   jax.ShapeDtypeStruct((ne, d), BF16), jax.ShapeDtypeStruct((ne, d), BF16)],
        scratch_shapes=[pltpu.VMEM((tb * W_SCRATCH_PITCH, LANES), F32)],
        compiler_params=pltpu.CompilerParams(
            dimension_semantics=("arbitrary",), vmem_limit_bytes=48 << 20),
        name="peer_w",
    )(i1, i2, g, peer_u, peer_v)


def _peer_ffn_kernel(hn_ref, u_ref, v_ref, w_ref, h_ref, o_ref):
    k = pl.program_id(1)
    nsub = w_ref.shape[0]

    @pl.when(k == 0)
    def _():
        o_ref[...] = h_ref[...]

    pre = _nt_dot(hn_ref[...], u_ref[...])
    act = 0.5 * pre * (1.0 + lax.erf(pre * (1.0 / math.sqrt(2.0))))
    w = jnp.concatenate([w_ref[s] for s in range(nsub)], axis=1).astype(F32)
    o_ref[...] += _dot((w * act).astype(BF16), v_ref[...])


def _peer_ffn(hn, ub, vb, w, h, tm=1024, te=1024):
    n = hn.shape[0]
    ne = ub.shape[0]
    return pl.pallas_call(
        _peer_ffn_kernel,
        grid=(n // tm, ne // te),
        in_specs=[pl.BlockSpec((tm, 1024), lambda i, k: (i, 0)),
                  pl.BlockSpec((te, 1024), lambda i, k: (k, 0)),
                  pl.BlockSpec((te, 1024), lambda i, k: (k, 0)),
                  pl.BlockSpec((te // N_KEYS, tm, LANES), lambda i, k: (k, i, 0)),
                  pl.BlockSpec((tm, 1024), lambda i, k: (i, 0))],
        out_specs=pl.BlockSpec((tm, 1024), lambda i, k: (i, 0)),
        out_shape=jax.ShapeDtypeStruct((n, 1024), F32),
        compiler_params=pltpu.CompilerParams(
            dimension_semantics=("arbitrary", "arbitrary"), vmem_limit_bytes=48 << 20),
        name="peer_ffn",
    )(hn, ub, vb, w, h)


def _rope_tables(seq, dim, width, reps):
    r = dim // ROPE_FRACTION
    half = r // 2
    inv_freq = ROPE_THETA ** (-jnp.arange(half, dtype=F32) / half)
    ang = jnp.arange(seq).astype(F32)[:, None] * inv_freq[None, :]
    cos, sin = jnp.cos(ang), jnp.sin(ang)
    rest1 = jnp.ones((seq, width - r), F32)
    rest0 = jnp.zeros((seq, width - r), F32)
    zh = jnp.zeros((seq, half), F32)
    c = jnp.concatenate([cos, cos, rest1], axis=1)
    sa = jnp.concatenate([-sin, zh, rest0], axis=1)
    sb = jnp.concatenate([zh, sin, rest0], axis=1)
    pad1 = jnp.ones((seq, LANES - width * reps), F32)
    pad0 = jnp.zeros((seq, LANES - width * reps), F32)
    c = jnp.concatenate([c] * reps + [pad1], axis=1)
    sa = jnp.concatenate([sa] * reps + [pad0], axis=1)
    sb = jnp.concatenate([sb] * reps + [pad0], axis=1)
    return jnp.stack([c, sa, sb])


def kernel(x, norm1_g, w_in, conv_w, q_norm_g, k_norm_g, w_o, norm2_g,
           peer_wq, peer_k1, peer_k2, peer_u, peer_v):
    batch, seq, d = x.shape
    n = batch * seq
    assert d == D_MODEL and seq % DSA_BLOCK == 0 and n % 1024 == 0, (batch, seq, d)
    x2 = x.reshape(n, d)

    assert sum(COL_SIZES) == w_in.shape[1] and sum(COL_SIZES[:7]) == MAIN_COLS
    gates_at = MAIN_COLS + IDX_DIM + IDX_HEADS
    pad = jnp.zeros((d, LANES - IDX_DIM - IDX_HEADS), w_in.dtype)
    w_tail = jnp.concatenate([w_in[:, gates_at:], w_in[:, MAIN_COLS:gates_at], pad], axis=1)

    wq_p = peer_wq.astype(BF16).reshape(d, PEER_HEADS, 2, PEER_HALF).transpose(0, 2, 1, 3).reshape(d, 2048)
    eye = jnp.eye(PEER_HEADS, dtype=BF16)

    def big(kk):
        return jnp.einsum('hnd,hg->nhgd', kk.astype(BF16), eye).reshape(
            N_KEYS * PEER_HEADS, PEER_HEADS * PEER_HALF)

    kbig = jnp.stack([big(peer_k1), big(peer_k2)])

    t128 = _rope_tables(seq, HEAD_DIM, HEAD_DIM, 1)
    t64 = _rope_tables(seq, IDX_DIM, IDX_DIM, 2)
    tki = _rope_tables(seq, IDX_DIM, IDX_DIM, 1)

    g1 = norm1_g.reshape(1, d)
    proj = _in_proj(x2, g1, w_in, MAIN_COLS, MAIN_COLS // 2, "in_proj")
    tail = _in_proj(x2, g1, w_tail, TAIL_COLS, TAIL_COLS, "in_proj_tail")
    qn, kn, vt, qir, kir, wist = _prep(proj, tail, t128, t64, tki, q_norm_g.reshape(1, HEAD_DIM),
                                       k_norm_g.reshape(1, HEAD_DIM), seq)
    yb = _dsa(qn, kn, vt, qir, kir, wist, batch, seq)
    h, hn, pq = _mix(proj, tail, yb, x2, conv_w, w_o.astype(BF16), norm2_g.reshape(1, d), wq_p, seq)
    i1, i2, g = _peer_topk(pq, kbig)
    w, ub, vb = _peer_w(i1, i2, g, peer_u, peer_v)
    out = _peer_ffn(hn, ub, vb, w, h)
    return out.reshape(batch, seq, d)
```

```python
import functools
import math

import jax
import jax.numpy as jnp
from jax import lax
from jax.experimental import pallas as pl
from jax.experimental.pallas import tpu as pltpu

F32 = jnp.float32
BF16 = jnp.bfloat16
I32 = jnp.int32

D_MODEL = 1024
EPS = 1e-6
CONV_K = 3
N_HEADS = 8
HEAD_DIM = 128
N_KV_HEADS = 2
GROUP = N_HEADS // N_KV_HEADS
KV_WIDTH = N_KV_HEADS * HEAD_DIM
IDX_HEADS = 8
IDX_DIM = 64
INDEX_TOPK = 256
ATTN_SCALE = HEAD_DIM ** -0.5
IDX_SCALE = IDX_DIM ** -0.5
IDX_W_SCALE = IDX_HEADS ** -0.5
ROPE_THETA = 500000.0
ROPE_FRACTION = 4
PEER_HEADS = 8
PEER_HALF = 128
N_KEYS = 128
PEER_TOPK = 16

LANES = 128
INT_MIN = -(2 ** 31)
KEY_NEG_INF = INT_MIN + 0x7FFFFF
NEG_INF = float("-inf")

COL_SIZES = (1024, 1024, 1024, 1024, 256, 256, 512, 64, 8, 1024, 1024)
OFF_B, OFF_C, OFF_XT, OFF_Q, OFF_K, OFF_V, OFF_QI = 0, 1024, 2048, 3072, 4096, 4352, 4608
MAIN_COLS = 5120
OFF_GA, OFF_GB, OFF_KIWI = 0, 1024, 2048
TAIL_COLS = 2176
W_SCRATCH_PITCH = 132
W_TOKENS_PER_ITER = 64
DSA_BLOCK = 256
M_FLOOR = -1e30

NT_DIMS = (((1,), (1,)), ((), ()))


def _nt_dot(a, b):
    return lax.dot_general(a, b, NT_DIMS, preferred_element_type=F32)


def _dot(a, b):
    return jnp.dot(a, b, preferred_element_type=F32)


def _in_proj_kernel(x_ref, g_ref, w_ref, o_ref):
    x = x_ref[...]
    xn = x * lax.rsqrt(jnp.mean(x * x, axis=-1, keepdims=True) + EPS) * g_ref[...]
    o_ref[...] = _dot(xn.astype(BF16), w_ref[...])


def _in_proj(x2, g, w, cols, tn, name, tm=512):
    n = x2.shape[0]
    return pl.pallas_call(
        _in_proj_kernel,
        grid=(cols // tn, n // tm),
        in_specs=[pl.BlockSpec((tm, D_MODEL), lambda j, i: (i, 0)),
                  pl.BlockSpec((1, D_MODEL), lambda j, i: (0, 0)),
                  pl.BlockSpec((D_MODEL, tn), lambda j, i: (0, j))],
        out_specs=pl.BlockSpec((tm, tn), lambda j, i: (i, j)),
        out_shape=jax.ShapeDtypeStruct((n, cols), F32),
        compiler_params=pltpu.CompilerParams(
            dimension_semantics=("arbitrary", "arbitrary"), vmem_limit_bytes=48 << 20),
        name=name,
    )(x2, g, w)


def _rope(x, c, sa, sb, shift):
    left = pltpu.roll(x, LANES - shift, 1)
    right = pltpu.roll(x, shift, 1)
    return x * c + left * sa + right * sb


def _head_norm(x, g):
    return x * lax.rsqrt(jnp.mean(x * x, axis=-1, keepdims=True) + EPS) * g


def _prep_kernel(q_ref, k_ref, v_ref, qi_ref, kiwi_ref, t128_ref, t64_ref, tki_ref,
                 qg_ref, kg_ref,
                 qn_ref, kn_ref, vt_ref, qir_ref, kir_ref, wist_ref):
    c128, sa128, sb128 = t128_ref[0], t128_ref[1], t128_ref[2]
    c64, sa64, sb64 = t64_ref[0], t64_ref[1], t64_ref[2]
    cki, saki, sbki = tki_ref[0], tki_ref[1], tki_ref[2]
    qg = qg_ref[...]
    kg = kg_ref[...]
    for h in range(N_HEADS):
        sl = slice(h * HEAD_DIM, (h + 1) * HEAD_DIM)
        qh = _head_norm(q_ref[:, sl], qg)
        qn_ref[:, sl] = _rope(qh, c128, sa128, sb128, 16).astype(BF16)
    for h in range(N_KV_HEADS):
        sl = slice(h * HEAD_DIM, (h + 1) * HEAD_DIM)
        kh = _head_norm(k_ref[:, sl], kg)
        kn_ref[:, sl] = _rope(kh, c128, sa128, sb128, 16).astype(BF16)
    vt_ref[0] = v_ref[...].T.astype(BF16)
    for p in range(IDX_HEADS // 2):
        sl = slice(p * LANES, (p + 1) * LANES)
        pair = _rope(qi_ref[:, sl], c64, sa64, sb64, 8).astype(BF16)
        qir_ref[0, 2 * p] = pair[:, :IDX_DIM]
        qir_ref[0, 2 * p + 1] = pair[:, IDX_DIM:]
    kiwi = kiwi_ref[...]
    kir_ref[...] = _rope(kiwi, cki, saki, sbki, 8)[:, :IDX_DIM].astype(BF16)
    wist_ref[...] = (kiwi * IDX_W_SCALE * IDX_SCALE).T[IDX_DIM:IDX_DIM + IDX_HEADS, :]


def _prep(proj, tail, t128, t64, tki, qg, kg, seq):
    tm = DSA_BLOCK
    n = proj.shape[0]
    sblk = seq // tm
    tab = lambda i: (0, i % sblk, 0)
    return pl.pallas_call(
        _prep_kernel,
        grid=(n // tm,),
        in_specs=[pl.BlockSpec((tm, 1024), lambda i: (i, OFF_Q // 1024)),
                  pl.BlockSpec((tm, 256), lambda i: (i, OFF_K // 256)),
                  pl.BlockSpec((tm, 256), lambda i: (i, OFF_V // 256)),
                  pl.BlockSpec((tm, 512), lambda i: (i, OFF_QI // 512)),
                  pl.BlockSpec((tm, 128), lambda i: (i, OFF_KIWI // 128)),
                  pl.BlockSpec((3, tm, 128), tab),
                  pl.BlockSpec((3, tm, 128), tab),
                  pl.BlockSpec((3, tm, 128), tab),
                  pl.BlockSpec((1, 128), lambda i: (0, 0)),
                  pl.BlockSpec((1, 128), lambda i: (0, 0))],
        out_specs=[pl.BlockSpec((tm, 1024), lambda i: (i, 0)),
                   pl.BlockSpec((tm, KV_WIDTH), lambda i: (i, 0)),
                   pl.BlockSpec((1, KV_WIDTH, tm), lambda i: (i, 0, 0)),
                   pl.BlockSpec((1, IDX_HEADS, tm, IDX_DIM), lambda i: (i, 0, 0, 0)),
                   pl.BlockSpec((tm, IDX_DIM), lambda i: (i, 0)),
                   pl.BlockSpec((IDX_HEADS, tm), lambda i: (0, i))],
        out_shape=[jax.ShapeDtypeStruct((n, 1024), BF16),
                   jax.ShapeDtypeStruct((n, KV_WIDTH), BF16),
                   jax.ShapeDtypeStruct((n // tm, KV_WIDTH, tm), BF16),
                   jax.ShapeDtypeStruct((n // tm, IDX_HEADS, tm, IDX_DIM), BF16),
                   jax.ShapeDtypeStruct((n, IDX_DIM), BF16),
                   jax.ShapeDtypeStruct((IDX_HEADS, n), F32)],
        compiler_params=pltpu.CompilerParams(dimension_semantics=("arbitrary",)),
        name="prep",
    )(proj, proj, proj, proj, tail, t128, t64, tki, qg, kg)


def _dsa_kernel(qn_ref, kn_ref, vt_ref, qir_ref, kir_ref, wist_ref, o_ref,
                keys_ref, bias_ref, acc_ref, *, k_sel):
    j = pl.program_id(1)
    nch = j + 1
    blk = DSA_BLOCK
    kf = float(k_sel)

    def rowsum(x):
        return jnp.sum(x, axis=0, keepdims=True)

    wist = wist_ref[...]
    qi = qir_ref[0].reshape(IDX_HEADS * blk, IDX_DIM)
    diff = lax.broadcasted_iota(I32, (blk, blk), 0) - lax.broadcasted_iota(I32, (blk, blk), 1)

    def score_chunk(c, carry):
        kc = kir_ref[pl.ds(pl.multiple_of(c * blk, blk), blk), :]
        d = jnp.maximum(_nt_dot(kc, qi), 0.0)
        acc = wist[0:1, :] * d[:, 0:blk]
        for h in range(1, IDX_HEADS):
            acc = acc + wist[h:h + 1, :] * d[:, h * blk:(h + 1) * blk]
        keys_ref[c] = jnp.where(diff <= (j - c) * blk, acc, NEG_INF)
        return carry

    lax.fori_loop(0, nch, score_chunk, 0)

    def count(pred):
        def one(c):
            m = jnp.where(pred(keys_ref[c]), 1.0, 0.0)
            return jnp.sum(m.reshape(blk // 8, 8, blk), axis=0)

        def body(c2, cnt):
            return cnt + (one(2 * c2) + one(2 * c2 + 1))

        cnt = lax.fori_loop(0, nch // 2, body, jnp.zeros((8, blk), F32))
        last = lax.cond(nch % 2 == 1, lambda: one(nch - 1), lambda: jnp.zeros((8, blk), F32))
        return rowsum(cnt + last)

    def as_float(t):
        t = jnp.maximum(t, KEY_NEG_INF)
        return pltpu.bitcast(t ^ ((t >> 31) & 0x7FFFFFFF), F32)

    c0 = count(lambda k: k >= 0.0)
    t0 = jnp.where(c0 >= kf, 0, INT_MIN).astype(I32)
    n0 = jnp.where(c0 >= kf, c0, (nch * blk).astype(F32))

    def bit_body(i, carry):
        t, n_ge = carry
        cand = t | jnp.left_shift(jnp.int32(1), 30 - i)
        cf = as_float(cand)
        cnt = count(lambda k: k >= cf)
        return jnp.where(cnt >= kf, cand, t), jnp.where(cnt >= kf, cnt, n_ge)

    thr_key, n_ge = lax.fori_loop(0, 31, bit_body, (t0, n0))
    thr = as_float(thr_key)
    ties = jnp.max(n_ge) > kf

    @pl.when(jnp.logical_not(ties))
    def _():
        def mask_chunk(c, carry):
            bias = jnp.where(keys_ref[c] >= thr, 0.0, NEG_INF)
            bias_ref[c] = jnp.where(diff <= (j - c) * blk, bias, NEG_INF)
            return carry

        lax.fori_loop(0, nch, mask_chunk, 0)

    @pl.when(ties)
    def _():
        need = kf - count(lambda k: k > thr)
        tri = jnp.where(diff >= 0, 1.0, 0.0).astype(BF16)

        def mask_chunk(c, seen):
            key = keys_ref[c]
            eq = jnp.where(key == thr, 1.0, 0.0)
            rank = seen + _dot(tri, eq.astype(BF16))
            tie_bias = jnp.where(rank <= need, 0.0, NEG_INF)
            eq_bias = jnp.where(eq > 0.0, tie_bias, NEG_INF)
            bias = jnp.where(key > thr, 0.0, eq_bias)
            bias_ref[c] = jnp.where(diff <= (j - c) * blk, bias, NEG_INF)
            return seen + rowsum(eq)

        lax.fori_loop(0, nch, mask_chunk, jnp.zeros((1, blk), F32))

    qs = [jnp.concatenate(
        [qn_ref[:, (g * GROUP + n) * HEAD_DIM:(g * GROUP + n + 1) * HEAD_DIM] for n in range(GROUP)],
        axis=0) for g in range(N_KV_HEADS)]
    acc_ref[...] = jnp.zeros(acc_ref.shape, F32)

    def attend(c, carry):
        rows = pl.ds(pl.multiple_of(c * blk, blk), blk)
        bias = jnp.concatenate([bias_ref[c]] * GROUP, axis=1)
        out = []
        for g in range(N_KV_HEADS):
            m, l = carry[g]
            gsl = slice(g * HEAD_DIM, (g + 1) * HEAD_DIM)
            s = _nt_dot(kn_ref[rows, gsl], qs[g]) * ATTN_SCALE + bias
            m_new = jnp.maximum(m, jnp.max(s, axis=0, keepdims=True))
            alpha = jnp.exp(m - m_new)
            p = jnp.exp(s - m_new)
            acc_ref[g] = acc_ref[g] * alpha + _dot(vt_ref[c, gsl, :], p.astype(BF16))
            out.append((m_new, l * alpha + rowsum(p)))
        return tuple(out)

    init = tuple((jnp.full((1, GROUP * blk), M_FLOOR, F32), jnp.zeros((1, GROUP * blk), F32))
                 for _ in range(N_KV_HEADS))
    stats = lax.fori_loop(0, nch, attend, init)
    for g in range(N_KV_HEADS):
        ot = acc_ref[g] / stats[g][1]
        for n in range(GROUP):
            hsl = slice((g * GROUP + n) * HEAD_DIM, (g * GROUP + n + 1) * HEAD_DIM)
            o_ref[:, hsl] = ot[:, n * blk:(n + 1) * blk].T


def _dsa(qn, kn, vt, qir, kir, wist, batch, seq):
    blk = DSA_BLOCK
    n = qn.shape[0]
    nblk = seq // blk
    k_sel = min(INDEX_TOPK, seq // 4)
    qblk = lambda b, j: (b * nblk + j, 0)
    full = lambda b, j: (b, 0)
    return pl.pallas_call(
        functools.partial(_dsa_kernel, k_sel=k_sel),
        grid=(batch, nblk),
        in_specs=[pl.BlockSpec((blk, 1024), qblk),
                  pl.BlockSpec((seq, KV_WIDTH), full),
                  pl.BlockSpec((nblk, KV_WIDTH, blk), lambda b, j: (b, 0, 0)),
                  pl.BlockSpec((1, IDX_HEADS, blk, IDX_DIM), lambda b, j: (b * nblk + j, 0, 0, 0)),
                  pl.BlockSpec((seq, IDX_DIM), full),
                  pl.BlockSpec((IDX_HEADS, blk), lambda b, j: (0, b * nblk + j))],
        out_specs=pl.BlockSpec((blk, 1024), qblk),
        out_shape=jax.ShapeDtypeStruct((n, 1024), F32),
        scratch_shapes=[pltpu.VMEM((nblk, blk, blk), F32),
                        pltpu.VMEM((nblk, blk, blk), F32),
                        pltpu.VMEM((N_KV_HEADS, HEAD_DIM, GROUP * blk), F32)],
        compiler_params=pltpu.CompilerParams(
            dimension_semantics=("arbitrary", "arbitrary"), vmem_limit_bytes=48 << 20),
        name="dsa",
    )(qn, kn, vt, qir, kir, wist)


def _mix_kernel(b_ref, c_ref, xt_ref, ch_ref, xth_ref, ga_ref, gb_ref, yb_ref, x_ref,
                cw_ref, wo_ref, g2_ref, wq_ref,
                h_ref, hn_ref, pq_ref, u_ref, *, blocks_per_seq):
    tm = b_ref.shape[0]
    i = pl.program_id(0)
    first = (i % blocks_per_seq) == 0
    halo = ch_ref[...] * xth_ref[...]
    u_ref[0:8, :] = jnp.where(first, 0.0, halo)
    u = c_ref[...] * xt_ref[...]
    u_ref[8:8 + tm, :] = u
    cw = cw_ref[...]
    conv = cw[0:1, :] * u_ref[6:6 + tm, :] + cw[1:2, :] * u_ref[7:7 + tm, :] + cw[2:3, :] * u
    y_a = b_ref[...] * conv
    merged = jax.nn.sigmoid(ga_ref[...]) * y_a + jax.nn.sigmoid(gb_ref[...]) * yb_ref[...]
    h = x_ref[...] + _dot(merged.astype(BF16), wo_ref[...])
    h_ref[...] = h
    hn = (h * lax.rsqrt(jnp.mean(h * h, axis=-1, keepdims=True) + EPS) * g2_ref[...]).astype(BF16)
    hn_ref[...] = hn
    pq_ref[...] = _dot(hn, wq_ref[...]).astype(BF16)


def _mix(proj, tail, yb, x2, conv_w, wo, g2, wq, seq, tm=256):
    n = x2.shape[0]
    col = lambda off: (lambda i: (i, off // 1024))
    halo = lambda off: (lambda i: (jnp.maximum(i * (tm // 8) - 1, 0), off // 1024))
    row = lambda i: (i, 0)
    const = lambda i: (0, 0)
    return pl.pallas_call(
        functools.partial(_mix_kernel, blocks_per_seq=seq // tm),
        grid=(n // tm,),
        in_specs=[pl.BlockSpec((tm, 1024), col(OFF_B)),
                  pl.BlockSpec((tm, 1024), col(OFF_C)),
                  pl.BlockSpec((tm, 1024), col(OFF_XT)),
                  pl.BlockSpec((8, 1024), halo(OFF_C)),
                  pl.BlockSpec((8, 1024), halo(OFF_XT)),
                  pl.BlockSpec((tm, 1024), col(OFF_GA)),
                  pl.BlockSpec((tm, 1024), col(OFF_GB)),
                  pl.BlockSpec((tm, 1024), row),
                  pl.BlockSpec((tm, 1024), row),
                  pl.BlockSpec((CONV_K, 1024), const),
                  pl.BlockSpec((1024, 1024), const),
                  pl.BlockSpec((1, 1024), const),
                  pl.BlockSpec((1024, 2048), const)],
        out_specs=[pl.BlockSpec((tm, 1024), row),
                   pl.BlockSpec((tm, 1024), row),
                   pl.BlockSpec((tm, 2048), row)],
        out_shape=[jax.ShapeDtypeStruct((n, 1024), F32),
                   jax.ShapeDtypeStruct((n, 1024), BF16),
                   jax.ShapeDtypeStruct((n, 2048), BF16)],
        scratch_shapes=[pltpu.VMEM((tm + 8, 1024), F32)],
        compiler_params=pltpu.CompilerParams(
            dimension_semantics=("arbitrary",), vmem_limit_bytes=48 << 20),
        name="mix",
    )(proj, proj, proj, proj, proj, tail, tail, yb, x2, conv_w, wo, g2, wq)


_CANDS = [(i, j) for i in range(PEER_TOPK) for j in range(PEER_TOPK) if (i + 1) * (j + 1) <= PEER_TOPK]


def _tree(op, xs):
    xs = list(xs)
    while len(xs) > 1:
        xs = [op(xs[i], xs[i + 1]) if i + 1 < len(xs) else xs[i] for i in range(0, len(xs), 2)]
    return xs[0]


def _oddeven_merge(lo, hi, r):
    step = r * 2
    if step < hi - lo:
        yield from _oddeven_merge(lo, hi, step)
        yield from _oddeven_merge(lo + r, hi, step)
        yield from [(i, i + r) for i in range(lo + r, hi - r, step)]
    else:
        yield (lo, lo + r)


def _oddeven_merge_sort(lo, hi):
    if hi - lo >= 1:
        mid = lo + (hi - lo) // 2
        yield from _oddeven_merge_sort(lo, mid)
        yield from _oddeven_merge_sort(mid + 1, hi)
        yield from _oddeven_merge(lo, hi, 1)


_SORT16 = list(_oddeven_merge_sort(0, PEER_TOPK - 1))
_BITONIC16 = [(i, i + d) for d in (8, 4, 2, 1) for i in range(PEER_TOPK) if not i & d]


def _hi(a, b):
    swap = b[0] > a[0]
    return (jnp.maximum(a[0], b[0]),) + tuple(jnp.where(swap, y, x) for x, y in zip(a[1:], b[1:]))


def _ce(a, b):
    swap = b[0] > a[0]
    lo = (jnp.minimum(a[0], b[0]),) + tuple(jnp.where(swap, x, y) for x, y in zip(a[1:], b[1:]))
    return _hi(a, b), lo


def _apply(items, pairs):
    for i, j in pairs:
        items[i], items[j] = _ce(items[i], items[j])
    return items


def _merge_top(a, b):
    m = len(b)
    c = [a[i] if i < PEER_TOPK - m else _hi(a[i], b[PEER_TOPK - 1 - i]) for i in range(PEER_TOPK)]
    return _apply(c, _BITONIC16)


def _peer_topk_kernel(pq_ref, kbig_ref, i1_ref, i2_ref, g_ref,
                      s_ref, val_ref, idx_ref, c_ref, top_ref, n1_ref, n2_ref):
    tm = pq_ref.shape[0]
    hp = PEER_HEADS
    halves = [slice(u * LANES, (u + 1) * LANES) for u in range(tm // LANES)]

    def any_true(flags):
        return jnp.max(_tree(jnp.maximum, flags)) > 0.0

    def tie_flag(top_vals, all_vals, ordered):
        cnt = _tree(jnp.add, [jnp.where(v >= top_vals[PEER_TOPK - 1], 1.0, 0.0) for v in all_vals])
        bad = jnp.where(cnt == float(PEER_TOPK), 0.0, 1.0)
        if ordered:
            bad = _tree(jnp.maximum, [bad] + [jnp.where(top_vals[r] > top_vals[r + 1], 0.0, 1.0)
                                              for r in range(PEER_TOPK - 1)])
        return bad

    for side in range(2):
        q = pq_ref[:, side * 1024:(side + 1) * 1024]
        s_ref[...] = _nt_dot(kbig_ref[side], q).reshape(N_KEYS, hp, tm)

        flags = []
        for ln in halves:
            rows = [s_ref[n, :, ln] for n in range(N_KEYS)]
            groups = [_apply([(rows[g * PEER_TOPK + r], g * PEER_TOPK + r) for r in range(PEER_TOPK)], _SORT16)
                      for g in range(N_KEYS // PEER_TOPK)]
            while len(groups) > 1:
                groups = [_merge_top(groups[u], groups[u + 1]) for u in range(0, len(groups), 2)]
            for r in range(PEER_TOPK):
                val_ref[side, r, :, ln] = groups[0][r][0]
                idx_ref[side, r, :, ln] = groups[0][r][1]
            flags.append(tie_flag([it[0] for it in groups[0]], rows, ordered=True))

        @pl.when(any_true(flags))
        def _():
            def extract(i, carry):
                rows = [s_ref[n] for n in range(N_KEYS)]
                m = _tree(jnp.maximum, rows)
                idx = _tree(jnp.minimum, [jnp.where(rows[n] == m, n, N_KEYS) for n in range(N_KEYS)])
                val_ref[side, i] = m
                idx_ref[side, i] = idx
                for n in range(N_KEYS):
                    s_ref[n] = jnp.where(idx == n, NEG_INF, rows[n])
                return carry

            lax.fori_loop(0, PEER_TOPK, extract, 0)

    flags = []
    for ln in halves:
        def cand(i, j):
            return (val_ref[0, i, :, ln] + val_ref[1, j, :, ln], idx_ref[0, i, :, ln], idx_ref[1, j, :, ln])

        g0 = [cand(0, j) for j in range(16)]
        g1 = _apply([cand(1, j) for j in range(8)] + [cand(i, 0) for i in range(15, 7, -1)], _BITONIC16)
        g2 = _apply([cand(i, j) for i, w in ((2, 5), (3, 4), (4, 3), (5, 2), (6, 2)) for j in range(w)], _SORT16)
        g3 = [cand(7, 0), cand(7, 1)]
        top = _merge_top(_merge_top(g0, g1), _merge_top(g2, g3))
        for k in range(PEER_TOPK):
            top_ref[k, :, ln] = top[k][0]
            n1_ref[k, :, ln] = top[k][1]
            n2_ref[k, :, ln] = top[k][2]
        flags.append(tie_flag([it[0] for it in top], [cand(i, j)[0] for i, j in _CANDS], ordered=False))

    @pl.when(any_true(flags))
    def _():
        for ci, (i, j) in enumerate(_CANDS):
            c_ref[ci] = val_ref[0, i] + val_ref[1, j]
        big = PEER_TOPK * PEER_TOPK

        def pick(k, carry):
            cs = [c_ref[ci] for ci in range(len(_CANDS))]
            m = _tree(jnp.maximum, cs)
            pos = _tree(jnp.minimum, [jnp.where(cs[ci] == m, i * PEER_TOPK + j, big)
                                      for ci, (i, j) in enumerate(_CANDS)])
            top_ref[k] = m
            isel = pos >> 4
            jsel = pos & (PEER_TOPK - 1)
            zero = jnp.zeros((hp, tm), I32)
            n1_ref[k] = _tree(jnp.add, [jnp.where(isel == r, idx_ref[0, r], zero) for r in range(PEER_TOPK)])
            n2_ref[k] = _tree(jnp.add, [jnp.where(jsel == r, idx_ref[1, r], zero) for r in range(PEER_TOPK)])
            for ci, (i, j) in enumerate(_CANDS):
                c_ref[ci] = jnp.where(pos == i * PEER_TOPK + j, NEG_INF, cs[ci])
            return carry

        lax.fori_loop(0, PEER_TOPK, pick, 0)

    top = top_ref[...]
    e = jnp.exp(top - top[0:1])
    gate = e / jnp.sum(e, axis=0, keepdims=True)
    i1_ref[...] = n1_ref[...].reshape(PEER_TOPK * hp, tm).T
    i2_ref[...] = n2_ref[...].reshape(PEER_TOPK * hp, tm).T
    g_ref[...] = gate.reshape(PEER_TOPK * hp, tm).T


def _peer_topk(pq, kbig, tm=256):
    n = pq.shape[0]
    r = PEER_TOPK * PEER_HEADS
    row = lambda i: (i, 0)
    ncand = len(_CANDS)
    return pl.pallas_call(
        _peer_topk_kernel,
        grid=(n // tm,),
        in_specs=[pl.BlockSpec((tm, 2048), row),
                  pl.BlockSpec((2, 1024, 1024), lambda i: (0, 0, 0))],
        out_specs=[pl.BlockSpec((tm, r), row)] * 3,
        out_shape=[jax.ShapeDtypeStruct((n, r), I32),
                   jax.ShapeDtypeStruct((n, r), I32),
                   jax.ShapeDtypeStruct((n, r), F32)],
        scratch_shapes=[pltpu.VMEM((N_KEYS, PEER_HEADS, tm), F32),
                        pltpu.VMEM((2, PEER_TOPK, PEER_HEADS, tm), F32),
                        pltpu.VMEM((2, PEER_TOPK, PEER_HEADS, tm), I32),
                        pltpu.VMEM((ncand, PEER_HEADS, tm), F32),
                        pltpu.VMEM((PEER_TOPK, PEER_HEADS, tm), F32),
                        pltpu.VMEM((PEER_TOPK, PEER_HEADS, tm), I32),
                        pltpu.VMEM((PEER_TOPK, PEER_HEADS, tm), I32)],
        compiler_params=pltpu.CompilerParams(dimension_semantics=("arbitrary",)),
        name="peer_topk",
    )(pq, kbig)


def _peer_w_kernel(i1_ref, i2_ref, g_ref, u_ref, v_ref, w_ref, ub_ref, vb_ref, scr_ref):
    tb = i1_ref.shape[0]
    ub_ref[...] = u_ref[...].astype(BF16)
    vb_ref[...] = v_ref[...].astype(BF16)
    sub = lax.broadcasted_iota(I32, (N_KEYS, LANES), 0)

    def token_group(tg, carry):
        for u in range(W_TOKENS_PER_ITER):
            t = tg * W_TOKENS_PER_ITER + u
            i1 = jnp.broadcast_to(i1_ref[pl.ds(t, 1), :], (N_KEYS, LANES))
            i2 = jnp.broadcast_to(i2_ref[pl.ds(t, 1), :], (N_KEYS, LANES))
            gv = jnp.broadcast_to(g_ref[pl.ds(t, 1), :], (N_KEYS, LANES))
            a = jnp.where(sub == i1, gv, 0.0).astype(BF16)
            bt = jnp.where(sub == i2, 1.0, 0.0).astype(BF16)
            scr_ref[pl.ds(t * W_SCRATCH_PITCH, N_KEYS), :] = _nt_dot(a, bt)
        return carry

    lax.fori_loop(0, tb // W_TOKENS_PER_ITER, token_group, 0)
    for n1 in range(N_KEYS):
        w_ref[n1] = scr_ref[pl.ds(n1, tb, stride=W_SCRATCH_PITCH), :].astype(BF16)


def _peer_w(i1, i2, g, peer_u, peer_v, tb=256):
    n = i1.shape[0]
    ne, d = peer_u.shape
    steps = n // tb
    assert ne % steps == 0, (ne, steps)
    er = ne // steps
    row = lambda i: (i, 0)
    return pl.pallas_call(
        _peer_w_kernel,
        grid=(steps,),
        in_specs=[pl.BlockSpec((tb, 128), row)] * 3 + [pl.BlockSpec((er, d), row)] * 2,
        out_specs=[pl.BlockSpec((N_KEYS, tb, LANES), lambda i: (0, i, 0)),
                   pl.BlockSpec((er, d), row), pl.BlockSpec((er, d), row)],
        out_shape=[jax.ShapeDtypeStruct((N_KEYS, n, LANES), BF16),
                   jax.ShapeDtypeStruct((ne, d), BF16), jax.ShapeDtypeStruct((ne, d), BF16)],
        scratch_shapes=[pltpu.VMEM((tb * W_SCRATCH_PITCH, LANES), F32)],
        compiler_params=pltpu.CompilerParams(
            dimension_semantics=("arbitrary",), vmem_limit_bytes=48 << 20),
        name="peer_w",
    )(i1, i2, g, peer_u, peer_v)


def _peer_ffn_kernel(hn_ref, u_ref, v_ref, w_ref, h_ref, o_ref):
    k = pl.program_id(1)
    nsub = w_ref.shape[0]

    @pl.when(k == 0)
    def _():
        o_ref[...] = h_ref[...]

    pre = _nt_dot(hn_ref[...], u_ref[...])
    act = 0.5 * pre * (1.0 + lax.erf(pre * (1.0 / math.sqrt(2.0))))
    w = jnp.concatenate([w_ref[s] for s in range(nsub)], axis=1).astype(F32)
    o_ref[...] += _dot((w * act).astype(BF16), v_ref[...])


def _peer_ffn(hn, ub, vb, w, h, tm=1024, te=1024):
    n = hn.shape[0]
    ne = ub.shape[0]
    return pl.pallas_call(
        _peer_ffn_kernel,
        grid=(n // tm, ne // te),
        in_specs=[pl.BlockSpec((tm, 1024), lambda i, k: (i, 0)),
                  pl.BlockSpec((te, 1024), lambda i, k: (k, 0)),
                  pl.BlockSpec((te, 1024), lambda i, k: (k, 0)),
                  pl.BlockSpec((te // N_KEYS, tm, LANES), lambda i, k: (k, i, 0)),
                  pl.BlockSpec((tm, 1024), lambda i, k: (i, 0))],
        out_specs=pl.BlockSpec((tm, 1024), lambda i, k: (i, 0)),
        out_shape=jax.ShapeDtypeStruct((n, 1024), F32),
        compiler_params=pltpu.CompilerParams(
            dimension_semantics=("arbitrary", "arbitrary"), vmem_limit_bytes=48 << 20),
        name="peer_ffn",
    )(hn, ub, vb, w, h)


def _rope_tables(seq, dim, width, reps):
    r = dim // ROPE_FRACTION
    half = r // 2
    inv_freq = ROPE_THETA ** (-jnp.arange(half, dtype=F32) / half)
    ang = jnp.arange(seq).astype(F32)[:, None] * inv_freq[None, :]
    cos, sin = jnp.cos(ang), jnp.sin(ang)
    rest1 = jnp.ones((seq, width - r), F32)
    rest0 = jnp.zeros((seq, width - r), F32)
    zh = jnp.zeros((seq, half), F32)
    c = jnp.concatenate([cos, cos, rest1], axis=1)
    sa = jnp.concatenate([-sin, zh, rest0], axis=1)
    sb = jnp.concatenate([zh, sin, rest0], axis=1)
    pad1 = jnp.ones((seq, LANES - width * reps), F32)
    pad0 = jnp.zeros((seq, LANES - width * reps), F32)
    c = jnp.concatenate([c] * reps + [pad1], axis=1)
    sa = jnp.concatenate([sa] * reps + [pad0], axis=1)
    sb = jnp.concatenate([sb] * reps + [pad0], axis=1)
    return jnp.stack([c, sa, sb])


def kernel(x, norm1_g, w_in, conv_w, q_norm_g, k_norm_g, w_o, norm2_g,
           peer_wq, peer_k1, peer_k2, peer_u, peer_v):
    batch, seq, d = x.shape
    n = batch * seq
    assert d == D_MODEL and seq % DSA_BLOCK == 0 and n % 1024 == 0, (batch, seq, d)
    x2 = x.reshape(n, d)

    assert sum(COL_SIZES) == w_in.shape[1] and sum(COL_SIZES[:7]) == MAIN_COLS
    w_in_b = w_in.astype(BF16)
    gates_at = MAIN_COLS + IDX_DIM + IDX_HEADS
    pad = jnp.zeros((d, LANES - IDX_DIM - IDX_HEADS), BF16)
    w_tail = jnp.concatenate([w_in_b[:, gates_at:], w_in_b[:, MAIN_COLS:gates_at], pad], axis=1)

    wq_p = peer_wq.astype(BF16).reshape(d, PEER_HEADS, 2, PEER_HALF).transpose(0, 2, 1, 3).reshape(d, 2048)
    eye = jnp.eye(PEER_HEADS, dtype=BF16)

    def big(kk):
        return jnp.einsum('hnd,hg->nhgd', kk.astype(BF16), eye).reshape(
            N_KEYS * PEER_HEADS, PEER_HEADS * PEER_HALF)

    kbig = jnp.stack([big(peer_k1), big(peer_k2)])

    t128 = _rope_tables(seq, HEAD_DIM, HEAD_DIM, 1)
    t64 = _rope_tables(seq, IDX_DIM, IDX_DIM, 2)
    tki = _rope_tables(seq, IDX_DIM, IDX_DIM, 1)

    g1 = norm1_g.reshape(1, d)
    proj = _in_proj(x2, g1, w_in_b, MAIN_COLS, MAIN_COLS // 2, "in_proj")
    tail = _in_proj(x2, g1, w_tail, TAIL_COLS, TAIL_COLS, "in_proj_tail")
    qn, kn, vt, qir, kir, wist = _prep(proj, tail, t128, t64, tki, q_norm_g.reshape(1, HEAD_DIM),
                                       k_norm_g.reshape(1, HEAD_DIM), seq)
    yb = _dsa(qn, kn, vt, qir, kir, wist, batch, seq)
    h, hn, pq = _mix(proj, tail, yb, x2, conv_w, w_o.astype(BF16), norm2_g.reshape(1, d), wq_p, seq)
    i1, i2, g = _peer_topk(pq, kbig)
    w, ub, vb = _peer_w(i1, i2, g, peer_u, peer_v)
    out = _peer_ffn(hn, ub, vb, w, h)
    return out.reshape(batch, seq, d)
```

```python
import functools
import math

import jax
import jax.numpy as jnp
from jax import lax
from jax.experimental import pallas as pl
from jax.experimental.pallas import tpu as pltpu

F32 = jnp.float32
BF16 = jnp.bfloat16
I32 = jnp.int32

D_MODEL = 1024
EPS = 1e-6
CONV_K = 3
N_HEADS = 8
HEAD_DIM = 128
N_KV_HEADS = 2
GROUP = N_HEADS // N_KV_HEADS
KV_WIDTH = N_KV_HEADS * HEAD_DIM
IDX_HEADS = 8
IDX_DIM = 64
INDEX_TOPK = 256
ATTN_SCALE = HEAD_DIM ** -0.5
IDX_SCALE = IDX_DIM ** -0.5
IDX_W_SCALE = IDX_HEADS ** -0.5
ROPE_THETA = 500000.0
ROPE_FRACTION = 4
PEER_HEADS = 8
PEER_HALF = 128
N_KEYS = 128
PEER_TOPK = 16

LANES = 128
INT_MIN = -(2 ** 31)
KEY_NEG_INF = INT_MIN + 0x7FFFFF
NEG_INF = float("-inf")

COL_SIZES = (1024, 1024, 1024, 1024, 256, 256, 512, 64, 8, 1024, 1024)
OFF_B, OFF_C, OFF_XT, OFF_Q, OFF_K, OFF_V, OFF_QI = 0, 1024, 2048, 3072, 4096, 4352, 4608
MAIN_COLS = 5120
OFF_GA, OFF_GB, OFF_KIWI = 0, 1024, 2048
TAIL_COLS = 2176
W_SCRATCH_PITCH = 132
W_TOKENS_PER_ITER = 128
DSA_BLOCK = 256
M_FLOOR = -1e30

NT_DIMS = (((1,), (1,)), ((), ()))


def _nt_dot(a, b):
    return lax.dot_general(a, b, NT_DIMS, preferred_element_type=F32)


def _dot(a, b):
    return jnp.dot(a, b, preferred_element_type=F32)


def _in_proj_kernel(x_ref, g_ref, w_ref, o_ref):
    x = x_ref[...]
    xn = x * lax.rsqrt(jnp.mean(x * x, axis=-1, keepdims=True) + EPS) * g_ref[...]
    o_ref[...] = _dot(xn.astype(BF16), w_ref[...])


def _in_proj(x2, g, w, cols, tn, name, tm=512):
    n = x2.shape[0]
    return pl.pallas_call(
        _in_proj_kernel,
        grid=(cols // tn, n // tm),
        in_specs=[pl.BlockSpec((tm, D_MODEL), lambda j, i: (i, 0)),
                  pl.BlockSpec((1, D_MODEL), lambda j, i: (0, 0)),
                  pl.BlockSpec((D_MODEL, tn), lambda j, i: (0, j))],
        out_specs=pl.BlockSpec((tm, tn), lambda j, i: (i, j)),
        out_shape=jax.ShapeDtypeStruct((n, cols), F32),
        compiler_params=pltpu.CompilerParams(
            dimension_semantics=("arbitrary", "arbitrary"), vmem_limit_bytes=48 << 20),
        name=name,
    )(x2, g, w)


def _rope(x, c, sa, sb, shift):
    left = pltpu.roll(x, LANES - shift, 1)
    right = pltpu.roll(x, shift, 1)
    return x * c + left * sa + right * sb


def _head_norm(x, g):
    return x * lax.rsqrt(jnp.mean(x * x, axis=-1, keepdims=True) + EPS) * g


def _prep_kernel(q_ref, k_ref, v_ref, qi_ref, kiwi_ref, t128_ref, t64_ref, tki_ref,
                 qg_ref, kg_ref,
                 qn_ref, kn_ref, vt_ref, qir_ref, kir_ref, wist_ref):
    c128, sa128, sb128 = t128_ref[0], t128_ref[1], t128_ref[2]
    c64, sa64, sb64 = t64_ref[0], t64_ref[1], t64_ref[2]
    cki, saki, sbki = tki_ref[0], tki_ref[1], tki_ref[2]
    qg = qg_ref[...]
    kg = kg_ref[...]
    for h in range(N_HEADS):
        sl = slice(h * HEAD_DIM, (h + 1) * HEAD_DIM)
        qh = _head_norm(q_ref[:, sl], qg)
        qn_ref[:, sl] = _rope(qh, c128, sa128, sb128, 16).astype(BF16)
    for h in range(N_KV_HEADS):
        sl = slice(h * HEAD_DIM, (h + 1) * HEAD_DIM)
        kh = _head_norm(k_ref[:, sl], kg)
        kn_ref[:, sl] = _rope(kh, c128, sa128, sb128, 16).astype(BF16)
    vt_ref[0] = v_ref[...].T.astype(BF16)
    for p in range(IDX_HEADS // 2):
        sl = slice(p * LANES, (p + 1) * LANES)
        pair = _rope(qi_ref[:, sl], c64, sa64, sb64, 8).astype(BF16)
        qir_ref[0, 2 * p] = pair[:, :IDX_DIM]
        qir_ref[0, 2 * p + 1] = pair[:, IDX_DIM:]
    kiwi = kiwi_ref[...]
    kir_ref[...] = _rope(kiwi, cki, saki, sbki, 8)[:, :IDX_DIM].astype(BF16)
    wist_ref[...] = (kiwi * IDX_W_SCALE * IDX_SCALE).T[IDX_DIM:IDX_DIM + IDX_HEADS, :]


def _prep(proj, tail, t128, t64, tki, qg, kg, seq):
    tm = DSA_BLOCK
    n = proj.shape[0]
    sblk = seq // tm
    tab = lambda i: (0, i % sblk, 0)
    return pl.pallas_call(
        _prep_kernel,
        grid=(n // tm,),
        in_specs=[pl.BlockSpec((tm, 1024), lambda i: (i, OFF_Q // 1024)),
                  pl.BlockSpec((tm, 256), lambda i: (i, OFF_K // 256)),
                  pl.BlockSpec((tm, 256), lambda i: (i, OFF_V // 256)),
                  pl.BlockSpec((tm, 512), lambda i: (i, OFF_QI // 512)),
                  pl.BlockSpec((tm, 128), lambda i: (i, OFF_KIWI // 128)),
                  pl.BlockSpec((3, tm, 128), tab),
                  pl.BlockSpec((3, tm, 128), tab),
                  pl.BlockSpec((3, tm, 128), tab),
                  pl.BlockSpec((1, 128), lambda i: (0, 0)),
                  pl.BlockSpec((1, 128), lambda i: (0, 0))],
        out_specs=[pl.BlockSpec((tm, 1024), lambda i: (i, 0)),
                   pl.BlockSpec((tm, KV_WIDTH), lambda i: (i, 0)),
                   pl.BlockSpec((1, KV_WIDTH, tm), lambda i: (i, 0, 0)),
                   pl.BlockSpec((1, IDX_HEADS, tm, IDX_DIM), lambda i: (i, 0, 0, 0)),
                   pl.BlockSpec((tm, IDX_DIM), lambda i: (i, 0)),
                   pl.BlockSpec((IDX_HEADS, tm), lambda i: (0, i))],
        out_shape=[jax.ShapeDtypeStruct((n, 1024), BF16),
                   jax.ShapeDtypeStruct((n, KV_WIDTH), BF16),
                   jax.ShapeDtypeStruct((n // tm, KV_WIDTH, tm), BF16),
                   jax.ShapeDtypeStruct((n // tm, IDX_HEADS, tm, IDX_DIM), BF16),
                   jax.ShapeDtypeStruct((n, IDX_DIM), BF16),
                   jax.ShapeDtypeStruct((IDX_HEADS, n), F32)],
        compiler_params=pltpu.CompilerParams(dimension_semantics=("arbitrary",)),
        name="prep",
    )(proj, proj, proj, proj, tail, t128, t64, tki, qg, kg)


def _dsa_kernel(qn_ref, kn_ref, vt_ref, qir_ref, kir_ref, wist_ref, o_ref,
                keys_ref, bias_ref, acc_ref, *, k_sel):
    j = pl.program_id(1)
    nch = j + 1
    blk = DSA_BLOCK
    kf = float(k_sel)

    def rowsum(x):
        return jnp.sum(x, axis=0, keepdims=True)

    wist = wist_ref[...]
    qi = qir_ref[0].reshape(IDX_HEADS * blk, IDX_DIM)
    diff = lax.broadcasted_iota(I32, (blk, blk), 0) - lax.broadcasted_iota(I32, (blk, blk), 1)

    def score_chunk(c, carry):
        kc = kir_ref[pl.ds(pl.multiple_of(c * blk, blk), blk), :]
        d = jnp.maximum(_nt_dot(kc, qi), 0.0)
        acc = wist[0:1, :] * d[:, 0:blk]
        for h in range(1, IDX_HEADS):
            acc = acc + wist[h:h + 1, :] * d[:, h * blk:(h + 1) * blk]
        keys_ref[c] = jnp.where(diff <= (j - c) * blk, acc, NEG_INF)
        return carry

    lax.fori_loop(0, nch, score_chunk, 0)

    def count(pred):
        def one(c):
            m = jnp.where(pred(keys_ref[c]), 1.0, 0.0)
            return jnp.sum(m.reshape(blk // 8, 8, blk), axis=0)

        def body(c2, cnt):
            return cnt + (one(2 * c2) + one(2 * c2 + 1))

        cnt = lax.fori_loop(0, nch // 2, body, jnp.zeros((8, blk), F32))
        last = lax.cond(nch % 2 == 1, lambda: one(nch - 1), lambda: jnp.zeros((8, blk), F32))
        return rowsum(cnt + last)

    def as_float(t):
        t = jnp.maximum(t, KEY_NEG_INF)
        return pltpu.bitcast(t ^ ((t >> 31) & 0x7FFFFFFF), F32)

    c0 = count(lambda k: k >= 0.0)
    t0 = jnp.where(c0 >= kf, 0, INT_MIN).astype(I32)
    n0 = jnp.where(c0 >= kf, c0, (nch * blk).astype(F32))

    def bit_body(i, carry):
        t, n_ge = carry
        cand = t | jnp.left_shift(jnp.int32(1), 30 - i)
        cf = as_float(cand)
        cnt = count(lambda k: k >= cf)
        return jnp.where(cnt >= kf, cand, t), jnp.where(cnt >= kf, cnt, n_ge)

    thr_key, n_ge = lax.fori_loop(0, 31, bit_body, (t0, n0))
    thr = as_float(thr_key)
    ties = jnp.max(n_ge) > kf

    @pl.when(jnp.logical_not(ties))
    def _():
        def mask_chunk(c, carry):
            bias = jnp.where(keys_ref[c] >= thr, 0.0, NEG_INF)
            bias_ref[c] = jnp.where(diff <= (j - c) * blk, bias, NEG_INF)
            return carry

        lax.fori_loop(0, nch, mask_chunk, 0)

    @pl.when(ties)
    def _():
        need = kf - count(lambda k: k > thr)
        tri = jnp.where(diff >= 0, 1.0, 0.0).astype(BF16)

        def mask_chunk(c, seen):
            key = keys_ref[c]
            eq = jnp.where(key == thr, 1.0, 0.0)
            rank = seen + _dot(tri, eq.astype(BF16))
            tie_bias = jnp.where(rank <= need, 0.0, NEG_INF)
            eq_bias = jnp.where(eq > 0.0, tie_bias, NEG_INF)
            bias = jnp.where(key > thr, 0.0, eq_bias)
            bias_ref[c] = jnp.where(diff <= (j - c) * blk, bias, NEG_INF)
            return seen + rowsum(eq)

        lax.fori_loop(0, nch, mask_chunk, jnp.zeros((1, blk), F32))

    qs = [jnp.concatenate(
        [qn_ref[:, (g * GROUP + n) * HEAD_DIM:(g * GROUP + n + 1) * HEAD_DIM] for n in range(GROUP)],
        axis=0) for g in range(N_KV_HEADS)]
    acc_ref[...] = jnp.zeros(acc_ref.shape, F32)

    def attend(c, carry):
        rows = pl.ds(pl.multiple_of(c * blk, blk), blk)
        bias = jnp.concatenate([bias_ref[c]] * GROUP, axis=1)
        out = []
        for g in range(N_KV_HEADS):
            m, l = carry[g]
            gsl = slice(g * HEAD_DIM, (g + 1) * HEAD_DIM)
            s = _nt_dot(kn_ref[rows, gsl], qs[g]) * ATTN_SCALE + bias
            m_new = jnp.maximum(m, jnp.max(s, axis=0, keepdims=True))
            alpha = jnp.exp(m - m_new)
            p = jnp.exp(s - m_new)
            acc_ref[g] = acc_ref[g] * alpha + _dot(vt_ref[c, gsl, :], p.astype(BF16))
            out.append((m_new, l * alpha + rowsum(p)))
        return tuple(out)

    init = tuple((jnp.full((1, GROUP * blk), M_FLOOR, F32), jnp.zeros((1, GROUP * blk), F32))
                 for _ in range(N_KV_HEADS))
    stats = lax.fori_loop(0, nch, attend, init)
    for g in range(N_KV_HEADS):
        ot = acc_ref[g] / stats[g][1]
        for n in range(GROUP):
            hsl = slice((g * GROUP + n) * HEAD_DIM, (g * GROUP + n + 1) * HEAD_DIM)
            o_ref[:, hsl] = ot[:, n * blk:(n + 1) * blk].T


def _dsa(qn, kn, vt, qir, kir, wist, batch, seq):
    blk = DSA_BLOCK
    n = qn.shape[0]
    nblk = seq // blk
    k_sel = min(INDEX_TOPK, seq // 4)
    qblk = lambda b, j: (b * nblk + j, 0)
    full = lambda b, j: (b, 0)
    return pl.pallas_call(
        functools.partial(_dsa_kernel, k_sel=k_sel),
        grid=(batch, nblk),
        in_specs=[pl.BlockSpec((blk, 1024), qblk),
                  pl.BlockSpec((seq, KV_WIDTH), full),
                  pl.BlockSpec((nblk, KV_WIDTH, blk), lambda b, j: (b, 0, 0)),
                  pl.BlockSpec((1, IDX_HEADS, blk, IDX_DIM), lambda b, j: (b * nblk + j, 0, 0, 0)),
                  pl.BlockSpec((seq, IDX_DIM), full),
                  pl.BlockSpec((IDX_HEADS, blk), lambda b, j: (0, b * nblk + j))],
        out_specs=pl.BlockSpec((blk, 1024), qblk),
        out_shape=jax.ShapeDtypeStruct((n, 1024), F32),
        scratch_shapes=[pltpu.VMEM((nblk, blk, blk), F32),
                        pltpu.VMEM((nblk, blk, blk), F32),
                        pltpu.VMEM((N_KV_HEADS, HEAD_DIM, GROUP * blk), F32)],
        compiler_params=pltpu.CompilerParams(
            dimension_semantics=("arbitrary", "arbitrary"), vmem_limit_bytes=48 << 20),
        name="dsa",
    )(qn, kn, vt, qir, kir, wist)


def _mix_kernel(b_ref, c_ref, xt_ref, ch_ref, xth_ref, ga_ref, gb_ref, yb_ref, x_ref,
                cw_ref, wo_ref, g2_ref, wq_ref,
                h_ref, hn_ref, pq_ref, u_ref, *, blocks_per_seq):
    tm = b_ref.shape[0]
    i = pl.program_id(0)
    first = (i % blocks_per_seq) == 0
    halo = ch_ref[...] * xth_ref[...]
    u_ref[0:8, :] = jnp.where(first, 0.0, halo)
    u = c_ref[...] * xt_ref[...]
    u_ref[8:8 + tm, :] = u
    cw = cw_ref[...]
    conv = cw[0:1, :] * u_ref[6:6 + tm, :] + cw[1:2, :] * u_ref[7:7 + tm, :] + cw[2:3, :] * u
    y_a = b_ref[...] * conv
    merged = jax.nn.sigmoid(ga_ref[...]) * y_a + jax.nn.sigmoid(gb_ref[...]) * yb_ref[...]
    h = x_ref[...] + _dot(merged.astype(BF16), wo_ref[...])
    h_ref[...] = h
    hn = (h * lax.rsqrt(jnp.mean(h * h, axis=-1, keepdims=True) + EPS) * g2_ref[...]).astype(BF16)
    hn_ref[...] = hn
    pq_ref[...] = _dot(hn, wq_ref[...]).astype(BF16)


def _mix(proj, tail, yb, x2, conv_w, wo, g2, wq, seq, tm=256):
    n = x2.shape[0]
    col = lambda off: (lambda i: (i, off // 1024))
    halo = lambda off: (lambda i: (jnp.maximum(i * (tm // 8) - 1, 0), off // 1024))
    row = lambda i: (i, 0)
    const = lambda i: (0, 0)
    return pl.pallas_call(
        functools.partial(_mix_kernel, blocks_per_seq=seq // tm),
        grid=(n // tm,),
        in_specs=[pl.BlockSpec((tm, 1024), col(OFF_B)),
                  pl.BlockSpec((tm, 1024), col(OFF_C)),
                  pl.BlockSpec((tm, 1024), col(OFF_XT)),
                  pl.BlockSpec((8, 1024), halo(OFF_C)),
                  pl.BlockSpec((8, 1024), halo(OFF_XT)),
                  pl.BlockSpec((tm, 1024), col(OFF_GA)),
                  pl.BlockSpec((tm, 1024), col(OFF_GB)),
                  pl.BlockSpec((tm, 1024), row),
                  pl.BlockSpec((tm, 1024), row),
                  pl.BlockSpec((CONV_K, 1024), const),
                  pl.BlockSpec((1024, 1024), const),
                  pl.BlockSpec((1, 1024), const),
                  pl.BlockSpec((1024, 2048), const)],
        out_specs=[pl.BlockSpec((tm, 1024), row),
                   pl.BlockSpec((tm, 1024), row),
                   pl.BlockSpec((tm, 2048), row)],
        out_shape=[jax.ShapeDtypeStruct((n, 1024), F32),
                   jax.ShapeDtypeStruct((n, 1024), BF16),
                   jax.ShapeDtypeStruct((n, 2048), BF16)],
        scratch_shapes=[pltpu.VMEM((tm + 8, 1024), F32)],
        compiler_params=pltpu.CompilerParams(
            dimension_semantics=("arbitrary",), vmem_limit_bytes=48 << 20),
        name="mix",
    )(proj, proj, proj, proj, proj, tail, tail, yb, x2, conv_w, wo, g2, wq)


_CANDS = [(i, j) for i in range(PEER_TOPK) for j in range(PEER_TOPK) if (i + 1) * (j + 1) <= PEER_TOPK]


def _tree(op, xs):
    xs = list(xs)
    while len(xs) > 1:
        xs = [op(xs[i], xs[i + 1]) if i + 1 < len(xs) else xs[i] for i in range(0, len(xs), 2)]
    return xs[0]


def _oddeven_merge(lo, hi, r):
    step = r * 2
    if step < hi - lo:
        yield from _oddeven_merge(lo, hi, step)
        yield from _oddeven_merge(lo + r, hi, step)
        yield from [(i, i + r) for i in range(lo + r, hi - r, step)]
    else:
        yield (lo, lo + r)


def _oddeven_merge_sort(lo, hi):
    if hi - lo >= 1:
        mid = lo + (hi - lo) // 2
        yield from _oddeven_merge_sort(lo, mid)
        yield from _oddeven_merge_sort(mid + 1, hi)
        yield from _oddeven_merge(lo, hi, 1)


_SORT16 = list(_oddeven_merge_sort(0, PEER_TOPK - 1))
_BITONIC16 = [(i, i + d) for d in (8, 4, 2, 1) for i in range(PEER_TOPK) if not i & d]


def _hi(a, b):
    swap = b[0] > a[0]
    return (jnp.maximum(a[0], b[0]),) + tuple(jnp.where(swap, y, x) for x, y in zip(a[1:], b[1:]))


def _ce(a, b):
    swap = b[0] > a[0]
    lo = (jnp.minimum(a[0], b[0]),) + tuple(jnp.where(swap, x, y) for x, y in zip(a[1:], b[1:]))
    return _hi(a, b), lo


def _apply(items, pairs):
    for i, j in pairs:
        items[i], items[j] = _ce(items[i], items[j])
    return items


def _merge_top(a, b):
    m = len(b)
    c = [a[i] if i < PEER_TOPK - m else _hi(a[i], b[PEER_TOPK - 1 - i]) for i in range(PEER_TOPK)]
    return _apply(c, _BITONIC16)


def _peer_topk_kernel(pq_ref, kbig_ref, i1_ref, i2_ref, g_ref,
                      s_ref, val_ref, idx_ref, c_ref, top_ref, n1_ref, n2_ref):
    tm = pq_ref.shape[0]
    hp = PEER_HEADS
    halves = [slice(u * LANES, (u + 1) * LANES) for u in range(tm // LANES)]

    def any_true(flags):
        return jnp.max(_tree(jnp.maximum, flags)) > 0.0

    def tie_flag(top_vals, all_vals, ordered):
        cnt = _tree(jnp.add, [jnp.where(v >= top_vals[PEER_TOPK - 1], 1.0, 0.0) for v in all_vals])
        bad = jnp.where(cnt == float(PEER_TOPK), 0.0, 1.0)
        if ordered:
            bad = _tree(jnp.maximum, [bad] + [jnp.where(top_vals[r] > top_vals[r + 1], 0.0, 1.0)
                                              for r in range(PEER_TOPK - 1)])
        return bad

    for side in range(2):
        q = pq_ref[:, side * 1024:(side + 1) * 1024]
        s_ref[...] = _nt_dot(kbig_ref[side], q).reshape(N_KEYS, hp, tm)

        flags = []
        for ln in halves:
            rows = [s_ref[n, :, ln] for n in range(N_KEYS)]
            groups = [_apply([(rows[g * PEER_TOPK + r], g * PEER_TOPK + r) for r in range(PEER_TOPK)], _SORT16)
                      for g in range(N_KEYS // PEER_TOPK)]
            while len(groups) > 1:
                groups = [_merge_top(groups[u], groups[u + 1]) for u in range(0, len(groups), 2)]
            for r in range(PEER_TOPK):
                val_ref[side, r, :, ln] = groups[0][r][0]
                idx_ref[side, r, :, ln] = groups[0][r][1]
            flags.append(tie_flag([it[0] for it in groups[0]], rows, ordered=True))

        @pl.when(any_true(flags))
        def _():
            def extract(i, carry):
                rows = [s_ref[n] for n in range(N_KEYS)]
                m = _tree(jnp.maximum, rows)
                idx = _tree(jnp.minimum, [jnp.where(rows[n] == m, n, N_KEYS) for n in range(N_KEYS)])
                val_ref[side, i] = m
                idx_ref[side, i] = idx
                for n in range(N_KEYS):
                    s_ref[n] = jnp.where(idx == n, NEG_INF, rows[n])
                return carry

            lax.fori_loop(0, PEER_TOPK, extract, 0)

    flags = []
    for ln in halves:
        def cand(i, j):
            return (val_ref[0, i, :, ln] + val_ref[1, j, :, ln], idx_ref[0, i, :, ln], idx_ref[1, j, :, ln])

        g0 = [cand(0, j) for j in range(16)]
        g1 = _apply([cand(1, j) for j in range(8)] + [cand(i, 0) for i in range(15, 7, -1)], _BITONIC16)
        g2 = _apply([cand(i, j) for i, w in ((2, 5), (3, 4), (4, 3), (5, 2), (6, 2)) for j in range(w)], _SORT16)
        g3 = [cand(7, 0), cand(7, 1)]
        top = _merge_top(_merge_top(g0, g1), _merge_top(g2, g3))
        for k in range(PEER_TOPK):
            top_ref[k, :, ln] = top[k][0]
            n1_ref[k, :, ln] = top[k][1]
            n2_ref[k, :, ln] = top[k][2]
        flags.append(tie_flag([it[0] for it in top], [cand(i, j)[0] for i, j in _CANDS], ordered=False))

    @pl.when(any_true(flags))
    def _():
        for ci, (i, j) in enumerate(_CANDS):
            c_ref[ci] = val_ref[0, i] + val_ref[1, j]
        big = PEER_TOPK * PEER_TOPK

        def pick(k, carry):
            cs = [c_ref[ci] for ci in range(len(_CANDS))]
            m = _tree(jnp.maximum, cs)
            pos = _tree(jnp.minimum, [jnp.where(cs[ci] == m, i * PEER_TOPK + j, big)
                                      for ci, (i, j) in enumerate(_CANDS)])
            top_ref[k] = m
            isel = pos >> 4
            jsel = pos & (PEER_TOPK - 1)
            zero = jnp.zeros((hp, tm), I32)
            n1_ref[k] = _tree(jnp.add, [jnp.where(isel == r, idx_ref[0, r], zero) for r in range(PEER_TOPK)])
            n2_ref[k] = _tree(jnp.add, [jnp.where(jsel == r, idx_ref[1, r], zero) for r in range(PEER_TOPK)])
            for ci, (i, j) in enumerate(_CANDS):
                c_ref[ci] = jnp.where(pos == i * PEER_TOPK + j, NEG_INF, cs[ci])
            return carry

        lax.fori_loop(0, PEER_TOPK, pick, 0)

    top = top_ref[...]
    e = jnp.exp(top - top[0:1])
    gate = e / jnp.sum(e, axis=0, keepdims=True)
    i1_ref[...] = n1_ref[...].reshape(PEER_TOPK * hp, tm).T
    i2_ref[...] = n2_ref[...].reshape(PEER_TOPK * hp, tm).T
    g_ref[...] = gate.reshape(PEER_TOPK * hp, tm).T


def _peer_topk(pq, kbig, tm=256):
    n = pq.shape[0]
    r = PEER_TOPK * PEER_HEADS
    row = lambda i: (i, 0)
    ncand = len(_CANDS)
    return pl.pallas_call(
        _peer_topk_kernel,
        grid=(n // tm,),
        in_specs=[pl.BlockSpec((tm, 2048), row),
                  pl.BlockSpec((2, 1024, 1024), lambda i: (0, 0, 0))],
        out_specs=[pl.BlockSpec((tm, r), row)] * 3,
        out_shape=[jax.ShapeDtypeStruct((n, r), I32),
                   jax.ShapeDtypeStruct((n, r), I32),
                   jax.ShapeDtypeStruct((n, r), F32)],
        scratch_shapes=[pltpu.VMEM((N_KEYS, PEER_HEADS, tm), F32),
                        pltpu.VMEM((2, PEER_TOPK, PEER_HEADS, tm), F32),
                        pltpu.VMEM((2, PEER_TOPK, PEER_HEADS, tm), I32),
                        pltpu.VMEM((ncand, PEER_HEADS, tm), F32),
                        pltpu.VMEM((PEER_TOPK, PEER_HEADS, tm), F32),
                        pltpu.VMEM((PEER_TOPK, PEER_HEADS, tm), I32),
                        pltpu.VMEM((PEER_TOPK, PEER_HEADS, tm), I32)],
        compiler_params=pltpu.CompilerParams(dimension_semantics=("arbitrary",)),
        name="peer_topk",
    )(pq, kbig)


def _peer_w_kernel(i1_ref, i2_ref, g_ref, u_ref, v_ref, w_ref, ub_ref, vb_ref, scr_ref):
    tb = i1_ref.shape[0]
    ub_ref[...] = u_ref[...].astype(BF16)
    vb_ref[...] = v_ref[...].astype(BF16)
    sub = lax.broadcasted_iota(I32, (N_KEYS, LANES), 0)

    def token_group(tg, carry):
        for u in range(W_TOKENS_PER_ITER):
            t = tg * W_TOKENS_PER_ITER + u
            i1 = jnp.broadcast_to(i1_ref[pl.ds(t, 1), :], (N_KEYS, LANES))
            i2 = jnp.broadcast_to(i2_ref[pl.ds(t, 1), :], (N_KEYS, LANES))
            gv = jnp.broadcast_to(g_ref[pl.ds(t, 1), :], (N_KEYS, LANES))
            a = jnp.where(sub == i1, gv, 0.0).astype(BF16)
            bt = jnp.where(sub == i2, 1.0, 0.0).astype(BF16)
            scr_ref[pl.ds(t * W_SCRATCH_PITCH, N_KEYS), :] = _nt_dot(a, bt)
        return carry

    lax.fori_loop(0, tb // W_TOKENS_PER_ITER, token_group, 0)
    for n1 in range(N_KEYS):
        w_ref[n1] = scr_ref[pl.ds(n1, tb, stride=W_SCRATCH_PITCH), :].astype(BF16)


def _peer_w(i1, i2, g, peer_u, peer_v, tb=256):
    n = i1.shape[0]
    ne, d = peer_u.shape
    steps = n // tb
    assert ne % steps == 0, (ne, steps)
    er = ne // steps
    row = lambda i: (i, 0)
    return pl.pallas_call(
        _peer_w_kernel,
        grid=(steps,),
        in_specs=[pl.BlockSpec((tb, 128), row)] * 3 + [pl.BlockSpec((er, d), row)] * 2,
        out_specs=[pl.BlockSpec((N_KEYS, tb, LANES), lambda i: (0, i, 0)),
                   pl.BlockSpec((er, d), row), pl.BlockSpec((er, d), row)],
        out_shape=[jax.ShapeDtypeStruct((N_KEYS, n, LANES), BF16),
                   jax.ShapeDtypeStruct((ne, d), BF16), jax.ShapeDtypeStruct((ne, d), BF16)],
        scratch_shapes=[pltpu.VMEM((tb * W_SCRATCH_PITCH, LANES), F32)],
        compiler_params=pltpu.CompilerParams(
            dimension_semantics=("arbitrary",), vmem_limit_bytes=48 << 20),
        name="peer_w",
    )(i1, i2, g, peer_u, peer_v)


def _peer_ffn_kernel(hn_ref, u_ref, v_ref, w_ref, h_ref, o_ref):
    k = pl.program_id(1)
    nsub = w_ref.shape[0]

    @pl.when(k == 0)
    def _():
        o_ref[...] = h_ref[...]

    pre = _nt_dot(hn_ref[...], u_ref[...])
    act = 0.5 * pre * (1.0 + lax.erf(pre * (1.0 / math.sqrt(2.0))))
    w = jnp.concatenate([w_ref[s] for s in range(nsub)], axis=1).astype(F32)
    o_ref[...] += _dot((w * act).astype(BF16), v_ref[...])


def _peer_ffn(hn, ub, vb, w, h, tm=1024, te=1024):
    n = hn.shape[0]
    ne = ub.shape[0]
    return pl.pallas_call(
        _peer_ffn_kernel,
        grid=(n // tm, ne // te),
        in_specs=[pl.BlockSpec((tm, 1024), lambda i, k: (i, 0)),
                  pl.BlockSpec((te, 1024), lambda i, k: (k, 0)),
                  pl.BlockSpec((te, 1024), lambda i, k: (k, 0)),
                  pl.BlockSpec((te // N_KEYS, tm, LANES), lambda i, k: (k, i, 0)),
                  pl.BlockSpec((tm, 1024), lambda i, k: (i, 0))],
        out_specs=pl.BlockSpec((tm, 1024), lambda i, k: (i, 0)),
        out_shape=jax.ShapeDtypeStruct((n, 1024), F32),
        compiler_params=pltpu.CompilerParams(
            dimension_semantics=("arbitrary", "arbitrary"), vmem_limit_bytes=48 << 20),
        name="peer_ffn",
    )(hn, ub, vb, w, h)


def _rope_tables(seq, dim, width, reps):
    r = dim // ROPE_FRACTION
    half = r // 2
    inv_freq = ROPE_THETA ** (-jnp.arange(half, dtype=F32) / half)
    ang = jnp.arange(seq).astype(F32)[:, None] * inv_freq[None, :]
    cos, sin = jnp.cos(ang), jnp.sin(ang)
    rest1 = jnp.ones((seq, width - r), F32)
    rest0 = jnp.zeros((seq, width - r), F32)
    zh = jnp.zeros((seq, half), F32)
    c = jnp.concatenate([cos, cos, rest1], axis=1)
    sa = jnp.concatenate([-sin, zh, rest0], axis=1)
    sb = jnp.concatenate([zh, sin, rest0], axis=1)
    pad1 = jnp.ones((seq, LANES - width * reps), F32)
    pad0 = jnp.zeros((seq, LANES - width * reps), F32)
    c = jnp.concatenate([c] * reps + [pad1], axis=1)
    sa = jnp.concatenate([sa] * reps + [pad0], axis=1)
    sb = jnp.concatenate([sb] * reps + [pad0], axis=1)
    return jnp.stack([c, sa, sb])


def kernel(x, norm1_g, w_in, conv_w, q_norm_g, k_norm_g, w_o, norm2_g,
           peer_wq, peer_k1, peer_k2, peer_u, peer_v):
    batch, seq, d = x.shape
    n = batch * seq
    assert d == D_MODEL and seq % DSA_BLOCK == 0 and n % 1024 == 0, (batch, seq, d)
    x2 = x.reshape(n, d)

    assert sum(COL_SIZES) == w_in.shape[1] and sum(COL_SIZES[:7]) == MAIN_COLS
    w_main = w_in[:, :MAIN_COLS].astype(BF16)
    gates_at = MAIN_COLS + IDX_DIM + IDX_HEADS
    pad = jnp.zeros((d, LANES - IDX_DIM - IDX_HEADS), w_in.dtype)
    w_tail = jnp.concatenate([w_in[:, gates_at:], w_in[:, MAIN_COLS:gates_at], pad], axis=1).astype(BF16)

    wq_p = peer_wq.astype(BF16).reshape(d, PEER_HEADS, 2, PEER_HALF).transpose(0, 2, 1, 3).reshape(d, 2048)
    eye = jnp.eye(PEER_HEADS, dtype=BF16)

    def big(kk):
        return jnp.einsum('hnd,hg->nhgd', kk.astype(BF16), eye).reshape(
            N_KEYS * PEER_HEADS, PEER_HEADS * PEER_HALF)

    kbig = jnp.stack([big(peer_k1), big(peer_k2)])

    t128 = _rope_tables(seq, HEAD_DIM, HEAD_DIM, 1)
    t64 = _rope_tables(seq, IDX_DIM, IDX_DIM, 2)
    tki = _rope_tables(seq, IDX_DIM, IDX_DIM, 1)

    g1 = norm1_g.reshape(1, d)
    proj = _in_proj(x2, g1, w_main, MAIN_COLS, MAIN_COLS // 2, "in_proj")
    tail = _in_proj(x2, g1, w_tail, TAIL_COLS, TAIL_COLS, "in_proj_tail")
    qn, kn, vt, qir, kir, wist = _prep(proj, tail, t128, t64, tki, q_norm_g.reshape(1, HEAD_DIM),
                                       k_norm_g.reshape(1, HEAD_DIM), seq)
    yb = _dsa(qn, kn, vt, qir, kir, wist, batch, seq)
    h, hn, pq = _mix(proj, tail, yb, x2, conv_w, w_o.astype(BF16), norm2_g.reshape(1, d), wq_p, seq)
    i1, i2, g = _peer_topk(pq, kbig)
    w, ub, vb = _peer_w(i1, i2, g, peer_u, peer_v)
    out = _peer_ffn(hn, ub, vb, w, h)
    return out.reshape(batch, seq, d)
```
